```python
import jax, jax.numpy as jnp
from jax import lax
import numpy as np

D_MODEL = 1024
BATCH = 8
SEQ = 2048
DEPTH = 1

D_MIX = 2 * D_MODEL
D_A = D_MIX // 2
D_B = D_MIX - D_A
A_GROUPS = 8
A_GROUP_DIM = D_A // A_GROUPS
A_CHUNK = 128
B_HEADS = 4
B_DK = D_B // 2 // B_HEADS
B_DV = D_B // B_HEADS
B_GATE_RANK = 16
B_GATE_TAU = 16.0
B_CHUNK = 64
D_IN = 3 * D_A + 2 * B_HEADS * B_DK + 2 * D_B + B_GATE_RANK
EPS = 1e-6

kernel_name = "hybrid_gmlp_gla_parallel_heads"


def rmsnorm(x, g):
    xf = x.astype(jnp.float32)
    y = xf * lax.rsqrt(jnp.mean(xf * xf, axis=-1, keepdims=True) + EPS)
    return (y * g.astype(jnp.float32)).astype(x.dtype)


def layernorm(x, g, b):
    xf = x.astype(jnp.float32)
    mu = jnp.mean(xf, axis=-1, keepdims=True)
    xc = xf - mu
    y = xc * lax.rsqrt(jnp.mean(xc * xc, axis=-1, keepdims=True) + EPS)
    return (y * g.astype(jnp.float32) + b.astype(jnp.float32)).astype(x.dtype)


def chunked_sgu(u, v, ln_g, ln_b, w_s, b_s):
    bsz, s = u.shape[:2]
    n = s // A_CHUNK
    vn = layernorm(v, ln_g, ln_b).reshape(bsz, n, A_CHUNK, A_GROUPS, A_GROUP_DIM)
    causal = jnp.tril(jnp.ones((A_CHUNK, A_CHUNK), dtype=bool))
    w = jnp.where(causal[None], w_s, jnp.zeros_like(w_s))
    sp = jnp.einsum('gts,bnsgc->bntgc', w, vn) + b_s.T[None, None, :, :, None]
    return u * sp.reshape(bsz, s, D_A).astype(u.dtype)


def gla_chunked(q, k, v, log_a):
    bsz, s = q.shape[:2]
    n = s // B_CHUNK
    f32 = jnp.float32

    def blk(t, d):
        return t.astype(f32).reshape(bsz, n, B_CHUNK, B_HEADS, d).transpose(0, 3, 1, 2, 4)

    q = blk(q, B_DK) * (B_DK ** -0.5)
    k = blk(k, B_DK)
    v = blk(v, B_DV)
    b = jnp.cumsum(blk(log_a, B_DK), axis=3)
    b_last = b[:, :, :, -1:, :]
    b_ref = b[:, :, :, B_CHUNK // 2 - 1:B_CHUNK // 2, :]

    qe = q * jnp.exp(b - b_ref)
    ke = k * jnp.exp(b_ref - b)
    scores = jnp.einsum('bhncd,bhnsd->bhncs', qe, ke)
    causal = jnp.tril(jnp.ones((B_CHUNK, B_CHUNK), dtype=bool))
    scores = jnp.where(causal, scores, 0.0)
    o_intra = jnp.einsum('bhncs,bhnsv->bhncv', scores, v)

    chunk_kv = jnp.einsum('bhncd,bhncv->bhndv', k * jnp.exp(b_last - b), v)
    decay = jnp.exp(b_last[:, :, :, 0, :])

    def step(state, inp):
        dec, kv = inp
        return dec[..., None] * state + kv, state

    s0 = jnp.zeros((bsz, B_HEADS, B_DK, B_DV), f32)
    _, s_prev = lax.scan(step, s0, (decay.transpose(2, 0, 1, 3), chunk_kv.transpose(2, 0, 1, 3, 4)))
    s_prev = s_prev.transpose(1, 2, 0, 3, 4)
    o_inter = jnp.einsum('bhncd,bhndv->bhncv', q * jnp.exp(b), s_prev)

    o = o_intra + o_inter
    return o.transpose(0, 2, 3, 1, 4).reshape(bsz, s, B_HEADS, B_DV)


def hybrid_layer(x, pre_g, w_in, w_a2, b_a2, a_ln_g, a_ln_b, a_w_s, a_b_s, b_norm_g, w_out, post_g):
    bsz, s, _ = x.shape
    h = rmsnorm(x, pre_g)
    z = jnp.einsum('bsd,de->bse', h, w_in)
    sizes = [D_A, D_A, D_A, B_HEADS * B_DK, B_HEADS * B_DK, D_B, D_B, B_GATE_RANK]
    cuts = np.cumsum(sizes)[:-1].tolist()
    a_u, a_v, a_gate, b_q, b_k, b_v, b_gate, b_lr = jnp.split(z, cuts, axis=-1)

    a_out = chunked_sgu(jax.nn.gelu(a_u, approximate=False), jax.nn.gelu(a_v, approximate=False),
                        a_ln_g, a_ln_b, a_w_s, a_b_s)
    a_out = a_out * jax.nn.silu(a_gate)

    gate_logit = jnp.einsum('bsr,re->bse', b_lr.astype(jnp.float32), w_a2.astype(jnp.float32)) + b_a2.astype(jnp.float32)
    log_a = jax.nn.log_sigmoid(gate_logit) / B_GATE_TAU
    o = gla_chunked(b_q.reshape(bsz, s, B_HEADS, B_DK), b_k.reshape(bsz, s, B_HEADS, B_DK),
                    b_v.reshape(bsz, s, B_HEADS, B_DV), log_a.reshape(bsz, s, B_HEADS, B_DK))
    o = rmsnorm(o, b_norm_g.reshape(B_HEADS, B_DV)).reshape(bsz, s, D_B).astype(x.dtype)
    b_out = o * jax.nn.silu(b_gate)

    mixed = jnp.concatenate([a_out, b_out], axis=-1)
    y = jnp.einsum('bse,ed->bsd', mixed, w_out)
    return x + rmsnorm(y, post_g)


def setup_inputs(seed: int = 0) -> dict:
    key = jax.random.key(seed)
    ks = jax.random.split(key, 13)
    f32 = jnp.float32
    nrm = lambda k, shp, sc: jax.random.normal(k, shp, f32) * sc
    return {
        "x": jax.random.normal(ks[0], (BATCH, SEQ, D_MODEL), f32),
        "pre_norm_g": 1.0 + nrm(ks[1], (DEPTH, D_MODEL), 0.02),
        "w_in": nrm(ks[2], (DEPTH, D_MODEL, D_IN), D_MODEL ** -0.5),
        "w_a2": nrm(ks[3], (DEPTH, B_GATE_RANK, B_HEADS * B_DK), B_GATE_RANK ** -0.5),
        "b_a2": nrm(ks[4], (DEPTH, B_HEADS * B_DK), 0.1),
        "a_ln_g": 1.0 + nrm(ks[5], (DEPTH, D_A), 0.02),
        "a_ln_b": nrm(ks[6], (DEPTH, D_A), 0.02),
        "a_w_s": nrm(ks[7], (DEPTH, A_GROUPS, A_CHUNK, A_CHUNK), A_CHUNK ** -0.5),
        "a_b_s": 1.0 + nrm(ks[8], (DEPTH, A_GROUPS, A_CHUNK), 0.02),
        "b_norm_g": 1.0 + nrm(ks[9], (DEPTH, D_B), 0.02),
        "w_out": nrm(ks[10], (DEPTH, D_MIX, D_MODEL), D_MIX ** -0.5),
        "post_norm_g": 1.0 + nrm(ks[11], (DEPTH, D_MODEL), 0.02),
    }


def reference(x, pre_norm_g, w_in, w_a2, b_a2, a_ln_g, a_ln_b, a_w_s, a_b_s, b_norm_g, w_out, post_norm_g):
    for l in range(DEPTH):
        x = hybrid_layer(x, pre_norm_g[l], w_in[l], w_a2[l], b_a2[l], a_ln_g[l], a_ln_b[l],
                         a_w_s[l], a_b_s[l], b_norm_g[l], w_out[l], post_norm_g[l])
    return x
```

```python
import functools

import jax
import jax.numpy as jnp
from jax import lax
from jax.experimental import pallas as pl
from jax.experimental.pallas import tpu as pltpu

D_MODEL = 1024
D_MIX = 2 * D_MODEL
D_A = D_MIX // 2
D_B = D_MIX - D_A
A_GROUPS = 8
A_GROUP_DIM = D_A // A_GROUPS
A_CHUNK = 128
B_HEADS = 4
B_DK = D_B // 2 // B_HEADS
B_DV = D_B // B_HEADS
B_GATE_RANK = 16
B_GATE_TAU = 16.0
B_CHUNK = 64
D_QK = B_HEADS * B_DK
EPS = 1e-6

OFF_U = 0
OFF_V = OFF_U + D_A
OFF_AG = OFF_V + D_A
OFF_Q = OFF_AG + D_A
OFF_K = OFF_Q + D_QK
OFF_BV = OFF_K + D_QK
OFF_BG = OFF_BV + D_B
OFF_LR = OFF_BG + D_B
D_IN = OFF_LR + B_GATE_RANK

V7X_LANES = 128
LR_PAD = V7X_LANES
D_IN_PAD = OFF_LR + LR_PAD

SEQ_TILE = 256
V7X_VMEM_LIMIT_BYTES = 56 * 1024 * 1024

F32 = jnp.float32
BF16 = jnp.bfloat16


def _gelu(x):
    return 0.5 * x * (1.0 + lax.erf(x * (0.5 ** 0.5)))


def _silu(x):
    return x * jax.nn.sigmoid(x)


def _log_sigmoid(x):
    return jnp.minimum(x, 0.0) - jnp.log1p(jnp.exp(-jnp.abs(x)))


def _rms_scale(x):
    return lax.rsqrt(jnp.mean(x * x, axis=-1, keepdims=True) + EPS)


def _layer_kernel(x_ref, pre_g_ref, w_in_ref, w_a2_ref, b_a2_ref, ln_g_ref, ln_b_ref,
                  ws_ref, bs_ref, bn_g_ref, w_out_ref, post_g_ref, o_ref,
                  h_ref, vn_ref, ug_ref, q_ref, k_ref, v_ref, bg_ref, bc_ref,
                  mixed_ref, state_ref):
    tile = x_ref.shape[0]

    @pl.when(pl.program_id(1) == 0)
    def _reset_state():
        state_ref[...] = jnp.zeros_like(state_ref)

    x = x_ref[...]
    h_ref[...] = (x * _rms_scale(x) * pre_g_ref[...]).astype(BF16)

    def proj(off, width):
        return jnp.dot(h_ref[...], w_in_ref[:, off:off + width],
                       preferred_element_type=F32)

    v_a = _gelu(proj(OFF_V, D_A))
    mu = jnp.mean(v_a, axis=-1, keepdims=True)
    v_c = v_a - mu
    v_n = v_c * lax.rsqrt(jnp.mean(v_c * v_c, axis=-1, keepdims=True) + EPS)
    vn_ref[...] = (v_n * ln_g_ref[...] + ln_b_ref[...]).astype(BF16)
    ug_ref[...] = _gelu(proj(OFF_U, D_A)) * _silu(proj(OFF_AG, D_A))

    row = lax.broadcasted_iota(jnp.int32, (A_CHUNK, A_CHUNK), 0)
    col = lax.broadcasted_iota(jnp.int32, (A_CHUNK, A_CHUNK), 1)
    causal_a = row >= col
    for g in range(A_GROUPS):
        w_g = jnp.where(causal_a, ws_ref[g], 0.0).astype(BF16)
        gc = slice(g * A_GROUP_DIM, (g + 1) * A_GROUP_DIM)
        for n in range(tile // A_CHUNK):
            rs = slice(n * A_CHUNK, (n + 1) * A_CHUNK)
            sp = jnp.dot(w_g, vn_ref[rs, gc], preferred_element_type=F32) + bs_ref[g]
            mixed_ref[rs, gc] = (ug_ref[rs, gc] * sp).astype(BF16)

    q_ref[...] = proj(OFF_Q, D_QK) * (B_DK ** -0.5)
    k_ref[...] = proj(OFF_K, D_QK)
    v_ref[...] = proj(OFF_BV, D_B)
    bg_ref[...] = _silu(proj(OFF_BG, D_B))
    lr = proj(OFF_LR, LR_PAD).astype(BF16)
    gate_logit = jnp.dot(lr, w_a2_ref[...], preferred_element_type=F32) + b_a2_ref[...]
    log_a = _log_sigmoid(gate_logit) * (1.0 / B_GATE_TAU)

    r_t = lax.broadcasted_iota(jnp.int32, (tile, tile), 0)
    c_t = lax.broadcasted_iota(jnp.int32, (tile, tile), 1)
    shift = B_CHUNK.bit_length() - 1
    cum_mat = ((r_t >= c_t) & ((r_t >> shift) == (c_t >> shift))).astype(BF16)
    la_hi = log_a.astype(BF16)
    la_lo = (log_a - la_hi.astype(F32)).astype(BF16)
    bc_ref[...] = (jnp.dot(cum_mat, la_hi, preferred_element_type=F32)
                   + jnp.dot(cum_mat, la_lo, preferred_element_type=F32))

    r_c = lax.broadcasted_iota(jnp.int32, (B_CHUNK, B_CHUNK), 0)
    c_c = lax.broadcasted_iota(jnp.int32, (B_CHUNK, B_CHUNK), 1)
    causal_b = r_c >= c_c
    nt = (((1,), (1,)), ((), ()))
    tn = (((0,), (0,)), ((), ()))
    for c in range(tile // B_CHUNK):
        r0 = c * B_CHUNK
        rs = slice(r0, r0 + B_CHUNK)
        for hd in range(B_HEADS):
            kc_ = slice(hd * B_DK, (hd + 1) * B_DK)
            vc_ = slice(hd * B_DV, (hd + 1) * B_DV)
            b = bc_ref[rs, kc_]
            b_last = bc_ref[r0 + B_CHUNK - 1:r0 + B_CHUNK, kc_]
            b_mid = bc_ref[r0 + B_CHUNK // 2 - 1:r0 + B_CHUNK // 2, kc_]
            q_c = q_ref[rs, kc_]
            k_c = k_ref[rs, kc_]
            v_c = v_ref[rs, vc_].astype(BF16)
            qe = (q_c * jnp.exp(b - b_mid)).astype(BF16)
            ke = (k_c * jnp.exp(b_mid - b)).astype(BF16)
            scores = lax.dot_general(qe, ke, nt, preferred_element_type=F32)
            scores = jnp.where(causal_b, scores, 0.0).astype(BF16)
            o = jnp.dot(scores, v_c, preferred_element_type=F32)
            s_t = state_ref[hd]
            qb = (q_c * jnp.exp(b)).astype(BF16)
            o = o + lax.dot_general(qb, s_t.astype(BF16), nt, preferred_element_type=F32)
            kd = (k_c * jnp.exp(b_last - b)).astype(BF16)
            kv_t = lax.dot_general(v_c, kd, tn, preferred_element_type=F32)
            state_ref[hd] = s_t * jnp.exp(b_last) + kv_t
            o_n = o * _rms_scale(o) * bn_g_ref[:, vc_]
            mixed_ref[rs, D_A + hd * B_DV:D_A + (hd + 1) * B_DV] = (
                o_n * bg_ref[rs, vc_]).astype(BF16)

    y = jnp.dot(mixed_ref[...], w_out_ref[...], preferred_element_type=F32)
    o_ref[...] = x_ref[...] + y * _rms_scale(y) * post_g_ref[...]


def _resident(shape):
    zeros = (0,) * len(shape)
    return pl.BlockSpec(shape, lambda b, s: zeros, pipeline_mode=pl.Buffered(1))


def _hybrid_layer(x, pre_g, w_in, w_a2, b_a2, ln_g, ln_b, w_s, b_s, bn_g, w_out, post_g):
    bsz, seq, d = x.shape
    assert d == D_MODEL and seq % SEQ_TILE == 0 and SEQ_TILE % A_CHUNK == 0
    assert w_in.shape == (D_MODEL, D_IN) and w_out.shape == (D_MIX, D_MODEL)

    w_in_p = jnp.pad(w_in, ((0, 0), (0, D_IN_PAD - D_IN))).astype(BF16)
    w_a2_p = jnp.pad(w_a2, ((0, LR_PAD - B_GATE_RANK), (0, 0))).astype(BF16)
    row_vec = lambda a: a.reshape(1, -1).astype(F32)

    x_spec = pl.BlockSpec((None, SEQ_TILE, D_MODEL), lambda b, s: (b, s, 0))
    scratch = [
        pltpu.VMEM((SEQ_TILE, D_MODEL), BF16),
        pltpu.VMEM((SEQ_TILE, D_A), BF16),
        pltpu.VMEM((SEQ_TILE, D_A), F32),
        pltpu.VMEM((SEQ_TILE, D_QK), F32),
        pltpu.VMEM((SEQ_TILE, D_QK), F32),
        pltpu.VMEM((SEQ_TILE, D_B), F32),
        pltpu.VMEM((SEQ_TILE, D_B), F32),
        pltpu.VMEM((SEQ_TILE, D_QK), F32),
        pltpu.VMEM((SEQ_TILE, D_MIX), BF16),
        pltpu.VMEM((B_HEADS, B_DV, B_DK), F32),
    ]
    return pl.pallas_call(
        _layer_kernel,
        out_shape=jax.ShapeDtypeStruct(x.shape, x.dtype),
        grid=(bsz, seq // SEQ_TILE),
        in_specs=[
            x_spec,
            _resident((1, D_MODEL)),
            _resident((D_MODEL, D_IN_PAD)),
            _resident((LR_PAD, D_QK)),
            _resident((1, D_QK)),
            _resident((1, D_A)),
            _resident((1, D_A)),
            _resident((A_GROUPS, A_CHUNK, A_CHUNK)),
            _resident((A_GROUPS, A_CHUNK, 1)),
            _resident((1, D_B)),
            _resident((D_MIX, D_MODEL)),
            _resident((1, D_MODEL)),
        ],
        out_specs=x_spec,
        scratch_shapes=scratch,
        compiler_params=pltpu.CompilerParams(
            dimension_semantics=("arbitrary", "arbitrary"),
            vmem_limit_bytes=V7X_VMEM_LIMIT_BYTES),
        name="hybrid_gmlp_gla_layer",
    )(x, row_vec(pre_g), w_in_p, w_a2_p, row_vec(b_a2), row_vec(ln_g), row_vec(ln_b),
      w_s.astype(F32), b_s.astype(F32).reshape(A_GROUPS, A_CHUNK, 1), row_vec(bn_g),
      w_out.astype(BF16), row_vec(post_g))


def kernel(x, pre_norm_g, w_in, w_a2, b_a2, a_ln_g, a_ln_b, a_w_s, a_b_s, b_norm_g, w_out, post_norm_g):
    for l in range(pre_norm_g.shape[0]):
        x = _hybrid_layer(x, pre_norm_g[l], w_in[l], w_a2[l], b_a2[l], a_ln_g[l], a_ln_b[l],
                          a_w_s[l], a_b_s[l], b_norm_g[l], w_out[l], post_norm_g[l])
    return x
```

```python
import jax
import jax.numpy as jnp
from jax import lax
from jax.experimental import pallas as pl
from jax.experimental.pallas import tpu as pltpu

D_MODEL = 1024
D_MIX = 2 * D_MODEL
D_A = D_MIX // 2
D_B = D_MIX - D_A
A_GROUPS = 8
A_GROUP_DIM = D_A // A_GROUPS
A_CHUNK = 128
B_HEADS = 4
B_DK = D_B // 2 // B_HEADS
B_DV = D_B // B_HEADS
B_GATE_RANK = 16
B_GATE_TAU = 16.0
B_CHUNK = 64
D_QK = B_HEADS * B_DK
EPS = 1e-6

OFF_U = 0
OFF_V = OFF_U + D_A
OFF_AG = OFF_V + D_A
OFF_Q = OFF_AG + D_A
OFF_K = OFF_Q + D_QK
OFF_BV = OFF_K + D_QK
OFF_BG = OFF_BV + D_B
OFF_LR = OFF_BG + D_B
D_IN = OFF_LR + B_GATE_RANK

V7X_LANES = 128
LR_PAD = V7X_LANES
D_IN_PAD = OFF_LR + LR_PAD

SEQ_TILE = 256
V7X_VMEM_LIMIT_BYTES = 56 * 1024 * 1024

F32 = jnp.float32
BF16 = jnp.bfloat16


def _gelu(x):
    return 0.5 * x * (1.0 + lax.erf(x * (0.5 ** 0.5)))


def _silu(x):
    return x * jax.nn.sigmoid(x)


def _log_sigmoid(x):
    return jnp.minimum(x, 0.0) - jnp.log1p(jnp.exp(-jnp.abs(x)))


def _rms_scale(x):
    return lax.rsqrt(jnp.mean(x * x, axis=-1, keepdims=True) + EPS)


def _layer_kernel(x_ref, pre_g_ref, w_in_ref, w_a2_ref, b_a2_ref, ln_g_ref, ln_b_ref,
                  ws_ref, bs_ref, bn_g_ref, w_out_ref, post_g_ref, o_ref,
                  h_ref, vn_ref, ug_ref, qe_ref, ke_ref, kd_ref, qb_ref, v_ref, bg_ref,
                  bc_ref, sprev_ref, mixed_ref, state_ref):
    tile = x_ref.shape[0]
    n_chunks = tile // B_CHUNK
    nt = (((1,), (1,)), ((), ()))
    tn = (((0,), (0,)), ((), ()))

    @pl.when(pl.program_id(1) == 0)
    def _reset_state():
        state_ref[...] = jnp.zeros_like(state_ref)

    x = x_ref[...]
    h_ref[...] = (x * _rms_scale(x) * pre_g_ref[...]).astype(BF16)

    def proj(off, width):
        return jnp.dot(h_ref[...], w_in_ref[:, off:off + width],
                       preferred_element_type=F32)

    lr = proj(OFF_LR, LR_PAD).astype(BF16)
    gate_logit = jnp.dot(lr, w_a2_ref[...], preferred_element_type=F32) + b_a2_ref[...]
    log_a = _log_sigmoid(gate_logit) * (1.0 / B_GATE_TAU)
    r_t = lax.broadcasted_iota(jnp.int32, (tile, tile), 0)
    c_t = lax.broadcasted_iota(jnp.int32, (tile, tile), 1)
    shift = B_CHUNK.bit_length() - 1
    same_chunk_causal = (r_t >= c_t) & ((r_t >> shift) == (c_t >> shift))
    cum_mat = same_chunk_causal.astype(BF16)
    la_hi = log_a.astype(BF16)
    la_lo = (log_a - la_hi.astype(F32)).astype(BF16)
    bc_ref[...] = (jnp.dot(cum_mat, la_hi, preferred_element_type=F32)
                   + jnp.dot(cum_mat, la_lo, preferred_element_type=F32))

    q = proj(OFF_Q, D_QK) * (B_DK ** -0.5)
    k = proj(OFF_K, D_QK)
    decays = []
    for c in range(n_chunks):
        r0 = c * B_CHUNK
        rs = slice(r0, r0 + B_CHUNK)
        b = bc_ref[rs, :]
        b_last = bc_ref[r0 + B_CHUNK - 1:r0 + B_CHUNK, :]
        b_mid = bc_ref[r0 + B_CHUNK // 2 - 1:r0 + B_CHUNK // 2, :]
        q_c = q[rs]
        k_c = k[rs]
        qe_ref[rs, :] = (q_c * jnp.exp(b - b_mid)).astype(BF16)
        ke_ref[rs, :] = (k_c * jnp.exp(b_mid - b)).astype(BF16)
        kd_ref[rs, :] = (k_c * jnp.exp(b_last - b)).astype(BF16)
        qb_ref[rs, :] = (q_c * jnp.exp(b)).astype(BF16)
        decays.append(jnp.exp(b_last))
    v_ref[...] = proj(OFF_BV, D_B).astype(BF16)
    bg_ref[...] = _silu(proj(OFF_BG, D_B))

    for hd in range(B_HEADS):
        kc_ = slice(hd * B_DK, (hd + 1) * B_DK)
        vc_ = slice(hd * B_DV, (hd + 1) * B_DV)
        s_t = state_ref[hd]
        for c in range(n_chunks):
            rs = slice(c * B_CHUNK, (c + 1) * B_CHUNK)
            sprev_ref[c, hd] = s_t.astype(BF16)
            kv_t = lax.dot_general(v_ref[rs, vc_], kd_ref[rs, kc_], tn,
                                   preferred_element_type=F32)
            s_t = s_t * decays[c][:, kc_] + kv_t
        state_ref[hd] = s_t
    for hd in range(B_HEADS):
        kc_ = slice(hd * B_DK, (hd + 1) * B_DK)
        vc_ = slice(hd * B_DV, (hd + 1) * B_DV)
        scores = lax.dot_general(qe_ref[:, kc_], ke_ref[:, kc_], nt,
                                 preferred_element_type=F32)
        scores = jnp.where(same_chunk_causal, scores, 0.0).astype(BF16)
        o_intra = jnp.dot(scores, v_ref[:, vc_], preferred_element_type=F32)
        o_inter = jnp.concatenate([
            lax.dot_general(qb_ref[c * B_CHUNK:(c + 1) * B_CHUNK, kc_], sprev_ref[c, hd], nt,
                            preferred_element_type=F32)
            for c in range(n_chunks)], axis=0)
        o = o_intra + o_inter
        o_n = o * _rms_scale(o) * bn_g_ref[:, vc_]
        mixed_ref[:, D_A + hd * B_DV:D_A + (hd + 1) * B_DV] = (
            o_n * bg_ref[:, vc_]).astype(BF16)

    v_a = _gelu(proj(OFF_V, D_A))
    mu = jnp.mean(v_a, axis=-1, keepdims=True)
    v_c = v_a - mu
    v_n = v_c * lax.rsqrt(jnp.mean(v_c * v_c, axis=-1, keepdims=True) + EPS)
    vn_ref[...] = (v_n * ln_g_ref[...] + ln_b_ref[...]).astype(BF16)
    ug_ref[...] = _gelu(proj(OFF_U, D_A)) * _silu(proj(OFF_AG, D_A))

    row = lax.broadcasted_iota(jnp.int32, (A_CHUNK, A_CHUNK), 0)
    col = lax.broadcasted_iota(jnp.int32, (A_CHUNK, A_CHUNK), 1)
    causal_a = row >= col
    for g in range(A_GROUPS):
        w_g = jnp.where(causal_a, ws_ref[g], 0.0).astype(BF16)
        gc = slice(g * A_GROUP_DIM, (g + 1) * A_GROUP_DIM)
        for n in range(tile // A_CHUNK):
            rs = slice(n * A_CHUNK, (n + 1) * A_CHUNK)
            sp = jnp.dot(w_g, vn_ref[rs, gc], preferred_element_type=F32) + bs_ref[g]
            mixed_ref[rs, gc] = (ug_ref[rs, gc] * sp).astype(BF16)

    y = jnp.dot(mixed_ref[...], w_out_ref[...], preferred_element_type=F32)
    o_ref[...] = x_ref[...] + y * _rms_scale(y) * post_g_ref[...]


def _resident(shape):
    zeros = (0,) * len(shape)
    return pl.BlockSpec(shape, lambda b, s: zeros, pipeline_mode=pl.Buffered(1))


def _hybrid_layer(x, pre_g, w_in, w_a2, b_a2, ln_g, ln_b, w_s, b_s, bn_g, w_out, post_g):
    bsz, seq, d = x.shape
    assert d == D_MODEL and seq % SEQ_TILE == 0 and SEQ_TILE % A_CHUNK == 0
    assert w_in.shape == (D_MODEL, D_IN) and w_out.shape == (D_MIX, D_MODEL)

    w_in_p = jnp.pad(w_in, ((0, 0), (0, D_IN_PAD - D_IN))).astype(BF16)
    w_a2_p = jnp.pad(w_a2, ((0, LR_PAD - B_GATE_RANK), (0, 0))).astype(BF16)
    row_vec = lambda a: a.reshape(1, -1).astype(F32)

    x_spec = pl.BlockSpec((None, SEQ_TILE, D_MODEL), lambda b, s: (b, s, 0))
    n_chunks = SEQ_TILE // B_CHUNK
    scratch = [
        pltpu.VMEM((SEQ_TILE, D_MODEL), BF16),
        pltpu.VMEM((SEQ_TILE, D_A), BF16),
        pltpu.VMEM((SEQ_TILE, D_A), F32),
        pltpu.VMEM((SEQ_TILE, D_QK), BF16),
        pltpu.VMEM((SEQ_TILE, D_QK), BF16),
        pltpu.VMEM((SEQ_TILE, D_QK), BF16),
        pltpu.VMEM((SEQ_TILE, D_QK), BF16),
        pltpu.VMEM((SEQ_TILE, D_B), BF16),
        pltpu.VMEM((SEQ_TILE, D_B), F32),
        pltpu.VMEM((SEQ_TILE, D_QK), F32),
        pltpu.VMEM((n_chunks, B_HEADS, B_DV, B_DK), BF16),
        pltpu.VMEM((SEQ_TILE, D_MIX), BF16),
        pltpu.VMEM((B_HEADS, B_DV, B_DK), F32),
    ]
    return pl.pallas_call(
        _layer_kernel,
        out_shape=jax.ShapeDtypeStruct(x.shape, x.dtype),
        grid=(bsz, seq // SEQ_TILE),
        in_specs=[
            x_spec,
            _resident((1, D_MODEL)),
            _resident((D_MODEL, D_IN_PAD)),
            _resident((LR_PAD, D_QK)),
            _resident((1, D_QK)),
            _resident((1, D_A)),
            _resident((1, D_A)),
            _resident((A_GROUPS, A_CHUNK, A_CHUNK)),
            _resident((A_GROUPS, A_CHUNK, 1)),
            _resident((1, D_B)),
            _resident((D_MIX, D_MODEL)),
            _resident((1, D_MODEL)),
        ],
        out_specs=x_spec,
        scratch_shapes=scratch,
        compiler_params=pltpu.CompilerParams(
            dimension_semantics=("arbitrary", "arbitrary"),
            vmem_limit_bytes=V7X_VMEM_LIMIT_BYTES),
        name="hybrid_gmlp_gla_layer",
    )(x, row_vec(pre_g), w_in_p, w_a2_p, row_vec(b_a2), row_vec(ln_g), row_vec(ln_b),
      w_s.astype(F32), b_s.astype(F32).reshape(A_GROUPS, A_CHUNK, 1), row_vec(bn_g),
      w_out.astype(BF16), row_vec(post_g))


def kernel(x, pre_norm_g, w_in, w_a2, b_a2, a_ln_g, a_ln_b, a_w_s, a_b_s, b_norm_g, w_out, post_norm_g):
    for l in range(pre_norm_g.shape[0]):
        x = _hybrid_layer(x, pre_norm_g[l], w_in[l], w_a2[l], b_a2[l], a_ln_g[l], a_ln_b[l],
                          a_w_s[l], a_b_s[l], b_norm_g[l], w_out[l], post_norm_g[l])
    return x
```

```python
import jax
import jax.numpy as jnp
from jax import lax
from jax.experimental import pallas as pl
from jax.experimental.pallas import tpu as pltpu

D_MODEL = 1024
D_MIX = 2 * D_MODEL
D_A = D_MIX // 2
D_B = D_MIX - D_A
A_GROUPS = 8
A_GROUP_DIM = D_A // A_GROUPS
A_CHUNK = 128
B_HEADS = 4
B_DK = D_B // 2 // B_HEADS
B_DV = D_B // B_HEADS
B_GATE_RANK = 16
B_GATE_TAU = 16.0
B_CHUNK = 64
D_QK = B_HEADS * B_DK
EPS = 1e-6

OFF_U = 0
OFF_V = OFF_U + D_A
OFF_AG = OFF_V + D_A
OFF_Q = OFF_AG + D_A
OFF_K = OFF_Q + D_QK
OFF_BV = OFF_K + D_QK
OFF_BG = OFF_BV + D_B
OFF_LR = OFF_BG + D_B
D_IN = OFF_LR + B_GATE_RANK

V7X_LANES = 128
LR_PAD = V7X_LANES
D_IN_PAD = OFF_LR + LR_PAD

SEQ_TILE = 256
V7X_VMEM_LIMIT_BYTES = 56 * 1024 * 1024

F32 = jnp.float32
BF16 = jnp.bfloat16


def _gelu(x):
    return 0.5 * x * (1.0 + lax.erf(x * (0.5 ** 0.5)))


def _silu(x):
    return x * jax.nn.sigmoid(x)


def _log_sigmoid(x):
    return jnp.minimum(x, 0.0) - jnp.log1p(jnp.exp(-jnp.abs(x)))


def _rms_scale(x):
    return lax.rsqrt(jnp.mean(x * x, axis=-1, keepdims=True) + EPS)


def _layer_kernel(x_ref, pre_g_ref, w_in_ref, w_a2_ref, b_a2_ref, ln_g_ref, ln_b_ref,
                  ws_ref, bs_ref, bn_g_ref, w_out_ref, post_g_ref, o_ref,
                  h_ref, vn_ref, ug_ref, qe_ref, ke_ref, kd_ref, qb_ref, v_ref, bg_ref,
                  bc_ref, sprev_ref, mixed_ref, state_ref):
    tile = x_ref.shape[0]
    n_chunks = tile // B_CHUNK
    nt = (((1,), (1,)), ((), ()))
    tn = (((0,), (0,)), ((), ()))

    @pl.when(pl.program_id(1) == 0)
    def _reset_state():
        state_ref[...] = jnp.zeros_like(state_ref)

    x = x_ref[...]
    h_ref[...] = (x * _rms_scale(x) * pre_g_ref[...]).astype(BF16)

    def proj(off, width):
        return jnp.dot(h_ref[...], w_in_ref[:, off:off + width],
                       preferred_element_type=F32)

    lr = proj(OFF_LR, LR_PAD).astype(BF16)

    v_a = _gelu(proj(OFF_V, D_A))
    gate_logit = jnp.dot(lr, w_a2_ref[...], preferred_element_type=F32) + b_a2_ref[...]
    u_a = _gelu(proj(OFF_U, D_A))
    mu = jnp.mean(v_a, axis=-1, keepdims=True)
    v_c = v_a - mu
    v_n = v_c * lax.rsqrt(jnp.mean(v_c * v_c, axis=-1, keepdims=True) + EPS)
    vn_ref[...] = (v_n * ln_g_ref[...] + ln_b_ref[...]).astype(BF16)
    ug_ref[...] = u_a * _silu(proj(OFF_AG, D_A))

    row = lax.broadcasted_iota(jnp.int32, (A_CHUNK, A_CHUNK), 0)
    col = lax.broadcasted_iota(jnp.int32, (A_CHUNK, A_CHUNK), 1)
    causal_a = row >= col
    for g in range(A_GROUPS):
        w_g = jnp.where(causal_a, ws_ref[g], 0.0).astype(BF16)
        gc = slice(g * A_GROUP_DIM, (g + 1) * A_GROUP_DIM)
        for n in range(tile // A_CHUNK):
            rs = slice(n * A_CHUNK, (n + 1) * A_CHUNK)
            sp = jnp.dot(w_g, vn_ref[rs, gc], preferred_element_type=F32) + bs_ref[g]
            mixed_ref[rs, gc] = (ug_ref[rs, gc] * sp).astype(BF16)

    log_a = _log_sigmoid(gate_logit) * (1.0 / B_GATE_TAU)
    q = proj(OFF_Q, D_QK) * (B_DK ** -0.5)
    k = proj(OFF_K, D_QK)
    r_t = lax.broadcasted_iota(jnp.int32, (tile, tile), 0)
    c_t = lax.broadcasted_iota(jnp.int32, (tile, tile), 1)
    shift = B_CHUNK.bit_length() - 1
    same_chunk_causal = (r_t >= c_t) & ((r_t >> shift) == (c_t >> shift))
    cum_mat = same_chunk_causal.astype(BF16)
    la_hi = log_a.astype(BF16)
    la_lo = (log_a - la_hi.astype(F32)).astype(BF16)
    bc_ref[...] = (jnp.dot(cum_mat, la_hi, preferred_element_type=F32)
                   + jnp.dot(cum_mat, la_lo, preferred_element_type=F32))

    y_a = jnp.dot(mixed_ref[:, :D_A], w_out_ref[:D_A, :], preferred_element_type=F32)

    v_ref[...] = proj(OFF_BV, D_B).astype(BF16)

    decays = []
    for c in range(n_chunks):
        r0 = c * B_CHUNK
        rs = slice(r0, r0 + B_CHUNK)
        b = bc_ref[rs, :]
        b_last = bc_ref[r0 + B_CHUNK - 1:r0 + B_CHUNK, :]
        b_mid = bc_ref[r0 + B_CHUNK // 2 - 1:r0 + B_CHUNK // 2, :]
        q_c = q[rs]
        k_c = k[rs]
        qe_ref[rs, :] = (q_c * jnp.exp(b - b_mid)).astype(BF16)
        ke_ref[rs, :] = (k_c * jnp.exp(b_mid - b)).astype(BF16)
        kd_ref[rs, :] = (k_c * jnp.exp(b_last - b)).astype(BF16)
        qb_ref[rs, :] = (q_c * jnp.exp(b)).astype(BF16)
        decays.append(jnp.exp(b_last))

    bg_ref[...] = _silu(proj(OFF_BG, D_B))

    for hd in range(B_HEADS):
        kc_ = slice(hd * B_DK, (hd + 1) * B_DK)
        vc_ = slice(hd * B_DV, (hd + 1) * B_DV)
        s_t = state_ref[hd]
        for c in range(n_chunks):
            rs = slice(c * B_CHUNK, (c + 1) * B_CHUNK)
            sprev_ref[c, hd] = s_t.astype(BF16)
            kv_t = lax.dot_general(v_ref[rs, vc_], kd_ref[rs, kc_], tn,
                                   preferred_element_type=F32)
            s_t = s_t * decays[c][:, kc_] + kv_t
        state_ref[hd] = s_t

    scores = []
    for hd in range(B_HEADS):
        kc_ = slice(hd * B_DK, (hd + 1) * B_DK)
        s_hd = lax.dot_general(qe_ref[:, kc_], ke_ref[:, kc_], nt,
                               preferred_element_type=F32)
        scores.append(jnp.where(same_chunk_causal, s_hd, 0.0).astype(BF16))

    for hd in range(B_HEADS):
        kc_ = slice(hd * B_DK, (hd + 1) * B_DK)
        vc_ = slice(hd * B_DV, (hd + 1) * B_DV)
        o_intra = jnp.dot(scores[hd], v_ref[:, vc_], preferred_element_type=F32)
        o_inter = jnp.concatenate([
            lax.dot_general(qb_ref[c * B_CHUNK:(c + 1) * B_CHUNK, kc_], sprev_ref[c, hd], nt,
                            preferred_element_type=F32)
            for c in range(n_chunks)], axis=0)
        o = o_intra + o_inter
        o_n = o * _rms_scale(o) * bn_g_ref[:, vc_]
        mixed_ref[:, D_A + hd * B_DV:D_A + (hd + 1) * B_DV] = (
            o_n * bg_ref[:, vc_]).astype(BF16)

    y = y_a + jnp.dot(mixed_ref[:, D_A:], w_out_ref[D_A:, :], preferred_element_type=F32)
    o_ref[...] = x_ref[...] + y * _rms_scale(y) * post_g_ref[...]


def _resident(shape):
    zeros = (0,) * len(shape)
    return pl.BlockSpec(shape, lambda b, s: zeros, pipeline_mode=pl.Buffered(1))


def _hybrid_layer(x, pre_g, w_in, w_a2, b_a2, ln_g, ln_b, w_s, b_s, bn_g, w_out, post_g):
    bsz, seq, d = x.shape
    assert d == D_MODEL and seq % SEQ_TILE == 0 and SEQ_TILE % A_CHUNK == 0
    assert w_in.shape == (D_MODEL, D_IN) and w_out.shape == (D_MIX, D_MODEL)

    w_in_p = jnp.pad(w_in, ((0, 0), (0, D_IN_PAD - D_IN))).astype(BF16)
    w_a2_p = jnp.pad(w_a2, ((0, LR_PAD - B_GATE_RANK), (0, 0))).astype(BF16)
    row_vec = lambda a: a.reshape(1, -1).astype(F32)

    x_spec = pl.BlockSpec((None, SEQ_TILE, D_MODEL), lambda b, s: (b, s, 0))
    n_chunks = SEQ_TILE // B_CHUNK
    scratch = [
        pltpu.VMEM((SEQ_TILE, D_MODEL), BF16),
        pltpu.VMEM((SEQ_TILE, D_A), BF16),
        pltpu.VMEM((SEQ_TILE, D_A), F32),
        pltpu.VMEM((SEQ_TILE, D_QK), BF16),
        pltpu.VMEM((SEQ_TILE, D_QK), BF16),
        pltpu.VMEM((SEQ_TILE, D_QK), BF16),
        pltpu.VMEM((SEQ_TILE, D_QK), BF16),
        pltpu.VMEM((SEQ_TILE, D_B), BF16),
        pltpu.VMEM((SEQ_TILE, D_B), F32),
        pltpu.VMEM((SEQ_TILE, D_QK), F32),
        pltpu.VMEM((n_chunks, B_HEADS, B_DV, B_DK), BF16),
        pltpu.VMEM((SEQ_TILE, D_MIX), BF16),
        pltpu.VMEM((B_HEADS, B_DV, B_DK), F32),
    ]
    return pl.pallas_call(
        _layer_kernel,
        out_shape=jax.ShapeDtypeStruct(x.shape, x.dtype),
        grid=(bsz, seq // SEQ_TILE),
        in_specs=[
            x_spec,
            _resident((1, D_MODEL)),
            _resident((D_MODEL, D_IN_PAD)),
            _resident((LR_PAD, D_QK)),
            _resident((1, D_QK)),
            _resident((1, D_A)),
            _resident((1, D_A)),
            _resident((A_GROUPS, A_CHUNK, A_CHUNK)),
            _resident((A_GROUPS, A_CHUNK, 1)),
            _resident((1, D_B)),
            _resident((D_MIX, D_MODEL)),
            _resident((1, D_MODEL)),
        ],
        out_specs=x_spec,
        scratch_shapes=scratch,
        compiler_params=pltpu.CompilerParams(
            dimension_semantics=("arbitrary", "arbitrary"),
            vmem_limit_bytes=V7X_VMEM_LIMIT_BYTES),
        name="hybrid_gmlp_gla_layer",
    )(x, row_vec(pre_g), w_in_p, w_a2_p, row_vec(b_a2), row_vec(ln_g), row_vec(ln_b),
      w_s.astype(F32), b_s.astype(F32).reshape(A_GROUPS, A_CHUNK, 1), row_vec(bn_g),
      w_out.astype(BF16), row_vec(post_g))


def kernel(x, pre_norm_g, w_in, w_a2, b_a2, a_ln_g, a_ln_b, a_w_s, a_b_s, b_norm_g, w_out, post_norm_g):
    for l in range(pre_norm_g.shape[0]):
        x = _hybrid_layer(x, pre_norm_g[l], w_in[l], w_a2[l], b_a2[l], a_ln_g[l], a_ln_b[l],
                          a_w_s[l], a_b_s[l], b_norm_g[l], w_out[l], post_norm_g[l])
    return x
```

```python
import jax
import jax.numpy as jnp
from jax import lax
from jax.experimental import pallas as pl
from jax.experimental.pallas import tpu as pltpu

D_MODEL = 1024
D_MIX = 2 * D_MODEL
D_A = D_MIX // 2
D_B = D_MIX - D_A
A_GROUPS = 8
A_GROUP_DIM = D_A // A_GROUPS
A_CHUNK = 128
B_HEADS = 4
B_DK = D_B // 2 // B_HEADS
B_DV = D_B // B_HEADS
B_GATE_RANK = 16
B_GATE_TAU = 16.0
B_CHUNK = 64
D_QK = B_HEADS * B_DK
EPS = 1e-6

OFF_U = 0
OFF_V = OFF_U + D_A
OFF_AG = OFF_V + D_A
OFF_Q = OFF_AG + D_A
OFF_K = OFF_Q + D_QK
OFF_BV = OFF_K + D_QK
OFF_BG = OFF_BV + D_B
OFF_LR = OFF_BG + D_B
D_IN = OFF_LR + B_GATE_RANK

V7X_LANES = 128
LR_PAD = V7X_LANES
D_IN_PAD = OFF_LR + LR_PAD

SEQ_TILE = 512
GLA_BLOCK = 256
V7X_VMEM_LIMIT_BYTES = 56 * 1024 * 1024

F32 = jnp.float32
BF16 = jnp.bfloat16


def _gelu(x):
    return 0.5 * x * (1.0 + lax.erf(x * (0.5 ** 0.5)))


def _silu(x):
    return x * jax.nn.sigmoid(x)


def _log_sigmoid(x):
    return jnp.minimum(x, 0.0) - jnp.log1p(jnp.exp(-jnp.abs(x)))


def _rms_scale(x):
    return lax.rsqrt(jnp.mean(x * x, axis=-1, keepdims=True) + EPS)


def _layer_kernel(x_ref, pre_g_ref, w_in_ref, w_a2_ref, b_a2_ref, ln_g_ref, ln_b_ref,
                  ws_ref, bs_ref, bn_g_ref, w_out_ref, post_g_ref, o_ref,
                  h_ref, vn_ref, ug_ref, qe_ref, ke_ref, kd_ref, qb_ref, v_ref, bg_ref,
                  bc_ref, sprev_ref, mixed_ref, state_ref):
    tile = x_ref.shape[0]
    n_chunks = tile // B_CHUNK
    blocks = [slice(i * GLA_BLOCK, (i + 1) * GLA_BLOCK) for i in range(tile // GLA_BLOCK)]
    nt = (((1,), (1,)), ((), ()))
    tn = (((0,), (0,)), ((), ()))

    @pl.when(pl.program_id(1) == 0)
    def _reset_state():
        state_ref[...] = jnp.zeros_like(state_ref)

    x = x_ref[...]
    h_ref[...] = (x * _rms_scale(x) * pre_g_ref[...]).astype(BF16)

    def proj(off, width):
        return jnp.dot(h_ref[...], w_in_ref[:, off:off + width],
                       preferred_element_type=F32)

    lr = proj(OFF_LR, LR_PAD).astype(BF16)

    v_a = _gelu(proj(OFF_V, D_A))
    gate_logit = jnp.dot(lr, w_a2_ref[...], preferred_element_type=F32) + b_a2_ref[...]
    u_a = _gelu(proj(OFF_U, D_A))
    mu = jnp.mean(v_a, axis=-1, keepdims=True)
    v_c = v_a - mu
    v_n = v_c * lax.rsqrt(jnp.mean(v_c * v_c, axis=-1, keepdims=True) + EPS)
    vn_ref[...] = (v_n * ln_g_ref[...] + ln_b_ref[...]).astype(BF16)
    ug_ref[...] = u_a * _silu(proj(OFF_AG, D_A))

    row = lax.broadcasted_iota(jnp.int32, (A_CHUNK, A_CHUNK), 0)
    col = lax.broadcasted_iota(jnp.int32, (A_CHUNK, A_CHUNK), 1)
    causal_a = row >= col
    for g in range(A_GROUPS):
        w_g = jnp.where(causal_a, ws_ref[g], 0.0).astype(BF16)
        gc = slice(g * A_GROUP_DIM, (g + 1) * A_GROUP_DIM)
        for n in range(tile // A_CHUNK):
            rs = slice(n * A_CHUNK, (n + 1) * A_CHUNK)
            sp = jnp.dot(w_g, vn_ref[rs, gc], preferred_element_type=F32) + bs_ref[g]
            mixed_ref[rs, gc] = (ug_ref[rs, gc] * sp).astype(BF16)

    log_a = _log_sigmoid(gate_logit) * (1.0 / B_GATE_TAU)
    q = proj(OFF_Q, D_QK) * (B_DK ** -0.5)
    k = proj(OFF_K, D_QK)
    r_t = lax.broadcasted_iota(jnp.int32, (GLA_BLOCK, GLA_BLOCK), 0)
    c_t = lax.broadcasted_iota(jnp.int32, (GLA_BLOCK, GLA_BLOCK), 1)
    shift = B_CHUNK.bit_length() - 1
    same_chunk_causal = (r_t >= c_t) & ((r_t >> shift) == (c_t >> shift))
    cum_mat = same_chunk_causal.astype(BF16)
    la_hi = log_a.astype(BF16)
    la_lo = (log_a - la_hi.astype(F32)).astype(BF16)
    for blk in blocks:
        bc_ref[blk, :] = (jnp.dot(cum_mat, la_hi[blk], preferred_element_type=F32)
                          + jnp.dot(cum_mat, la_lo[blk], preferred_element_type=F32))

    y_a = jnp.dot(mixed_ref[:, :D_A], w_out_ref[:D_A, :], preferred_element_type=F32)

    v_ref[...] = proj(OFF_BV, D_B).astype(BF16)

    decays = []
    for c in range(n_chunks):
        r0 = c * B_CHUNK
        rs = slice(r0, r0 + B_CHUNK)
        b = bc_ref[rs, :]
        b_last = bc_ref[r0 + B_CHUNK - 1:r0 + B_CHUNK, :]
        b_mid = bc_ref[r0 + B_CHUNK // 2 - 1:r0 + B_CHUNK // 2, :]
        q_c = q[rs]
        k_c = k[rs]
        qe_ref[rs, :] = (q_c * jnp.exp(b - b_mid)).astype(BF16)
        ke_ref[rs, :] = (k_c * jnp.exp(b_mid - b)).astype(BF16)
        kd_ref[rs, :] = (k_c * jnp.exp(b_last - b)).astype(BF16)
        qb_ref[rs, :] = (q_c * jnp.exp(b)).astype(BF16)
        decays.append(jnp.exp(b_last))

    bg_ref[...] = _silu(proj(OFF_BG, D_B))

    for hd in range(B_HEADS):
        kc_ = slice(hd * B_DK, (hd + 1) * B_DK)
        vc_ = slice(hd * B_DV, (hd + 1) * B_DV)
        s_t = state_ref[hd]
        for c in range(n_chunks):
            rs = slice(c * B_CHUNK, (c + 1) * B_CHUNK)
            sprev_ref[c, hd] = s_t.astype(BF16)
            kv_t = lax.dot_general(v_ref[rs, vc_], kd_ref[rs, kc_], tn,
                                   preferred_element_type=F32)
            s_t = s_t * decays[c][:, kc_] + kv_t
        state_ref[hd] = s_t

    scores = {}
    for bi, blk in enumerate(blocks):
        for hd in range(B_HEADS):
            kc_ = slice(hd * B_DK, (hd + 1) * B_DK)
            s_hd = lax.dot_general(qe_ref[blk, kc_], ke_ref[blk, kc_], nt,
                                   preferred_element_type=F32)
            scores[bi, hd] = jnp.where(same_chunk_causal, s_hd, 0.0).astype(BF16)

    chunks_per_block = GLA_BLOCK // B_CHUNK
    for bi, blk in enumerate(blocks):
        for hd in range(B_HEADS):
            kc_ = slice(hd * B_DK, (hd + 1) * B_DK)
            vc_ = slice(hd * B_DV, (hd + 1) * B_DV)
            o_intra = jnp.dot(scores[bi, hd], v_ref[blk, vc_], preferred_element_type=F32)
            o_inter = jnp.concatenate([
                lax.dot_general(qb_ref[c * B_CHUNK:(c + 1) * B_CHUNK, kc_], sprev_ref[c, hd],
                                nt, preferred_element_type=F32)
                for c in range(bi * chunks_per_block, (bi + 1) * chunks_per_block)], axis=0)
            o = o_intra + o_inter
            o_n = o * _rms_scale(o) * bn_g_ref[:, vc_]
            mixed_ref[blk, D_A + hd * B_DV:D_A + (hd + 1) * B_DV] = (
                o_n * bg_ref[blk, vc_]).astype(BF16)

    y = y_a + jnp.dot(mixed_ref[:, D_A:], w_out_ref[D_A:, :], preferred_element_type=F32)
    o_ref[...] = x_ref[...] + y * _rms_scale(y) * post_g_ref[...]


def _resident(shape):
    zeros = (0,) * len(shape)
    return pl.BlockSpec(shape, lambda b, s: zeros, pipeline_mode=pl.Buffered(1))


def _hybrid_layer(x, pre_g, w_in, w_a2, b_a2, ln_g, ln_b, w_s, b_s, bn_g, w_out, post_g):
    bsz, seq, d = x.shape
    assert d == D_MODEL and seq % SEQ_TILE == 0
    assert SEQ_TILE % A_CHUNK == 0 and SEQ_TILE % GLA_BLOCK == 0 and GLA_BLOCK % B_CHUNK == 0
    assert w_in.shape == (D_MODEL, D_IN) and w_out.shape == (D_MIX, D_MODEL)

    w_in_p = jnp.pad(w_in, ((0, 0), (0, D_IN_PAD - D_IN))).astype(BF16)
    w_a2_p = jnp.pad(w_a2, ((0, LR_PAD - B_GATE_RANK), (0, 0))).astype(BF16)
    row_vec = lambda a: a.reshape(1, -1).astype(F32)

    x_spec = pl.BlockSpec((None, SEQ_TILE, D_MODEL), lambda b, s: (b, s, 0))
    n_chunks = SEQ_TILE // B_CHUNK
    scratch = [
        pltpu.VMEM((SEQ_TILE, D_MODEL), BF16),
        pltpu.VMEM((SEQ_TILE, D_A), BF16),
        pltpu.VMEM((SEQ_TILE, D_A), F32),
        pltpu.VMEM((SEQ_TILE, D_QK), BF16),
        pltpu.VMEM((SEQ_TILE, D_QK), BF16),
        pltpu.VMEM((SEQ_TILE, D_QK), BF16),
        pltpu.VMEM((SEQ_TILE, D_QK), BF16),
        pltpu.VMEM((SEQ_TILE, D_B), BF16),
        pltpu.VMEM((SEQ_TILE, D_B), F32),
        pltpu.VMEM((SEQ_TILE, D_QK), F32),
        pltpu.VMEM((n_chunks, B_HEADS, B_DV, B_DK), BF16),
        pltpu.VMEM((SEQ_TILE, D_MIX), BF16),
        pltpu.VMEM((B_HEADS, B_DV, B_DK), F32),
    ]
    return pl.pallas_call(
        _layer_kernel,
        out_shape=jax.ShapeDtypeStruct(x.shape, x.dtype),
        grid=(bsz, seq // SEQ_TILE),
        in_specs=[
            x_spec,
            _resident((1, D_MODEL)),
            _resident((D_MODEL, D_IN_PAD)),
            _resident((LR_PAD, D_QK)),
            _resident((1, D_QK)),
            _resident((1, D_A)),
            _resident((1, D_A)),
            _resident((A_GROUPS, A_CHUNK, A_CHUNK)),
            _resident((A_GROUPS, A_CHUNK, 1)),
            _resident((1, D_B)),
            _resident((D_MIX, D_MODEL)),
            _resident((1, D_MODEL)),
        ],
        out_specs=x_spec,
        scratch_shapes=scratch,
        compiler_params=pltpu.CompilerParams(
            dimension_semantics=("arbitrary", "arbitrary"),
            vmem_limit_bytes=V7X_VMEM_LIMIT_BYTES),
        name="hybrid_gmlp_gla_layer",
    )(x, row_vec(pre_g), w_in_p, w_a2_p, row_vec(b_a2), row_vec(ln_g), row_vec(ln_b),
      w_s.astype(F32), b_s.astype(F32).reshape(A_GROUPS, A_CHUNK, 1), row_vec(bn_g),
      w_out.astype(BF16), row_vec(post_g))


def kernel(x, pre_norm_g, w_in, w_a2, b_a2, a_ln_g, a_ln_b, a_w_s, a_b_s, b_norm_g, w_out, post_norm_g):
    for l in range(pre_norm_g.shape[0]):
        x = _hybrid_layer(x, pre_norm_g[l], w_in[l], w_a2[l], b_a2[l], a_ln_g[l], a_ln_b[l],
                          a_w_s[l], a_b_s[l], b_norm_g[l], w_out[l], post_norm_g[l])
    return x
```

```python
import jax
import jax.numpy as jnp
from jax import lax
from jax.experimental import pallas as pl
from jax.experimental.pallas import tpu as pltpu

D_MODEL = 1024
D_MIX = 2 * D_MODEL
D_A = D_MIX // 2
D_B = D_MIX - D_A
A_GROUPS = 8
A_GROUP_DIM = D_A // A_GROUPS
A_CHUNK = 128
B_HEADS = 4
B_DK = D_B // 2 // B_HEADS
B_DV = D_B // B_HEADS
B_GATE_RANK = 16
B_GATE_TAU = 16.0
B_CHUNK = 64
D_QK = B_HEADS * B_DK
EPS = 1e-6

OFF_U = 0
OFF_V = OFF_U + D_A
OFF_AG = OFF_V + D_A
OFF_Q = OFF_AG + D_A
OFF_K = OFF_Q + D_QK
OFF_BV = OFF_K + D_QK
OFF_BG = OFF_BV + D_B
OFF_LR = OFF_BG + D_B
D_IN = OFF_LR + B_GATE_RANK

V7X_LANES = 128
LR_PAD = V7X_LANES
D_IN_PAD = OFF_LR + LR_PAD

SEQ_TILE = 512
GLA_BLOCK = 256
V7X_VMEM_LIMIT_BYTES = 56 * 1024 * 1024

PHASE_ORDER = (
    "gate_lowrank", "sgu_value", "gate_log_decay", "sgu_u", "sgu_gate", "sgu_mix",
    "gla_q", "gla_k", "gla_cumsum", "out_a", "gla_v", "gla_operands", "gla_gate",
    "gla_state", "gla_scores", "gla_output", "out_b",
)

F32 = jnp.float32
BF16 = jnp.bfloat16


def _gelu(x):
    return 0.5 * x * (1.0 + lax.erf(x * (0.5 ** 0.5)))


def _silu(x):
    half = 0.5 * x
    return half + half * jnp.tanh(half)


def _log_sigmoid(x):
    return jnp.minimum(x, 0.0) - jnp.log(1.0 + jnp.exp(-jnp.abs(x)))


def _rms_scale(x):
    return lax.rsqrt(jnp.mean(x * x, axis=-1, keepdims=True) + EPS)


def _layer_kernel(x_ref, pre_g_ref, w_in_ref, w_a2_ref, b_a2_ref, ln_g_ref, ln_b_ref,
                  ws_ref, bs_ref, bn_g_ref, w_out_ref, post_g_ref, o_ref,
                  h_ref, vn_ref, ug_ref, qe_ref, ke_ref, kd_ref, qb_ref, v_ref, bg_ref,
                  bc_ref, sprev_ref, mixed_ref, state_ref):
    tile = x_ref.shape[0]
    n_chunks = tile // B_CHUNK
    blocks = [slice(i * GLA_BLOCK, (i + 1) * GLA_BLOCK) for i in range(tile // GLA_BLOCK)]
    nt = (((1,), (1,)), ((), ()))
    tn = (((0,), (0,)), ((), ()))

    @pl.when(pl.program_id(1) == 0)
    def _reset_state():
        state_ref[...] = jnp.zeros_like(state_ref)

    x = x_ref[...]
    h_ref[...] = (x * _rms_scale(x) * pre_g_ref[...]).astype(BF16)

    def proj(off, width):
        return jnp.dot(h_ref[...], w_in_ref[:, off:off + width],
                       preferred_element_type=F32)

    val = {}

    def gate_lowrank():
        val["lr"] = proj(OFF_LR, LR_PAD).astype(BF16)

    def gate_log_decay():
        gate_logit = (jnp.dot(val["lr"], w_a2_ref[...], preferred_element_type=F32)
                      + b_a2_ref[...])
        val["log_a"] = _log_sigmoid(gate_logit) * (1.0 / B_GATE_TAU)

    def gla_q():
        val["q"] = proj(OFF_Q, D_QK) * (B_DK ** -0.5)

    def gla_k():
        val["k"] = proj(OFF_K, D_QK)

    r_t = lax.broadcasted_iota(jnp.int32, (GLA_BLOCK, GLA_BLOCK), 0)
    c_t = lax.broadcasted_iota(jnp.int32, (GLA_BLOCK, GLA_BLOCK), 1)
    shift = B_CHUNK.bit_length() - 1
    same_chunk_causal = (r_t >= c_t) & ((r_t >> shift) == (c_t >> shift))

    def gla_cumsum():
        cum_mat = same_chunk_causal.astype(BF16)
        log_a = val["log_a"]
        la_hi = log_a.astype(BF16)
        la_lo = (log_a - la_hi.astype(F32)).astype(BF16)
        for blk in blocks:
            bc_ref[blk, :] = (jnp.dot(cum_mat, la_hi[blk], preferred_element_type=F32)
                              + jnp.dot(cum_mat, la_lo[blk], preferred_element_type=F32))

    def gla_v():
        v_ref[...] = proj(OFF_BV, D_B).astype(BF16)

    def gla_operands():
        q, k = val["q"], val["k"]
        decays = []
        for c in range(n_chunks):
            r0 = c * B_CHUNK
            rs = slice(r0, r0 + B_CHUNK)
            b = bc_ref[rs, :]
            b_last = bc_ref[r0 + B_CHUNK - 1:r0 + B_CHUNK, :]
            b_mid = bc_ref[r0 + B_CHUNK // 2 - 1:r0 + B_CHUNK // 2, :]
            q_c = q[rs]
            k_c = k[rs]
            qe_ref[rs, :] = (q_c * jnp.exp(b - b_mid)).astype(BF16)
            ke_ref[rs, :] = (k_c * jnp.exp(b_mid - b)).astype(BF16)
            kd_ref[rs, :] = (k_c * jnp.exp(b_last - b)).astype(BF16)
            qb_ref[rs, :] = (q_c * jnp.exp(b)).astype(BF16)
            decays.append(jnp.exp(b_last))
        val["decays"] = decays

    def gla_gate():
        bg_ref[...] = _silu(proj(OFF_BG, D_B))

    def gla_state():
        for hd in range(B_HEADS):
            kc_ = slice(hd * B_DK, (hd + 1) * B_DK)
            vc_ = slice(hd * B_DV, (hd + 1) * B_DV)
            s_t = state_ref[hd]
            for c in range(n_chunks):
                rs = slice(c * B_CHUNK, (c + 1) * B_CHUNK)
                sprev_ref[c, hd] = s_t.astype(BF16)
                kv_t = lax.dot_general(v_ref[rs, vc_], kd_ref[rs, kc_], tn,
                                       preferred_element_type=F32)
                s_t = s_t * val["decays"][c][:, kc_] + kv_t
            state_ref[hd] = s_t

    def gla_scores():
        scores = {}
        for bi, blk in enumerate(blocks):
            for hd in range(B_HEADS):
                kc_ = slice(hd * B_DK, (hd + 1) * B_DK)
                s_hd = lax.dot_general(qe_ref[blk, kc_], ke_ref[blk, kc_], nt,
                                       preferred_element_type=F32)
                scores[bi, hd] = jnp.where(same_chunk_causal, s_hd, 0.0).astype(BF16)
        val["scores"] = scores

    def gla_output():
        chunks_per_block = GLA_BLOCK // B_CHUNK
        for bi, blk in enumerate(blocks):
            for hd in range(B_HEADS):
                kc_ = slice(hd * B_DK, (hd + 1) * B_DK)
                vc_ = slice(hd * B_DV, (hd + 1) * B_DV)
                o_intra = jnp.dot(val["scores"][bi, hd], v_ref[blk, vc_],
                                  preferred_element_type=F32)
                o_inter = jnp.concatenate([
                    lax.dot_general(qb_ref[c * B_CHUNK:(c + 1) * B_CHUNK, kc_],
                                    sprev_ref[c, hd], nt, preferred_element_type=F32)
                    for c in range(bi * chunks_per_block, (bi + 1) * chunks_per_block)], axis=0)
                o = o_intra + o_inter
                o_n = o * _rms_scale(o) * bn_g_ref[:, vc_]
                mixed_ref[blk, D_A + hd * B_DV:D_A + (hd + 1) * B_DV] = (
                    o_n * bg_ref[blk, vc_]).astype(BF16)

    def sgu_value():
        v_a = _gelu(proj(OFF_V, D_A))
        mu = jnp.mean(v_a, axis=-1, keepdims=True)
        v_c = v_a - mu
        v_n = v_c * lax.rsqrt(jnp.mean(v_c * v_c, axis=-1, keepdims=True) + EPS)
        vn_ref[...] = (v_n * ln_g_ref[...] + ln_b_ref[...]).astype(BF16)

    def sgu_u():
        val["u_a"] = _gelu(proj(OFF_U, D_A))

    def sgu_gate():
        ug_ref[...] = val["u_a"] * _silu(proj(OFF_AG, D_A))

    def sgu_mix():
        row = lax.broadcasted_iota(jnp.int32, (A_CHUNK, A_CHUNK), 0)
        col = lax.broadcasted_iota(jnp.int32, (A_CHUNK, A_CHUNK), 1)
        causal_a = row >= col
        for g in range(A_GROUPS):
            w_g = jnp.where(causal_a, ws_ref[g], 0.0).astype(BF16)
            gc = slice(g * A_GROUP_DIM, (g + 1) * A_GROUP_DIM)
            for n in range(tile // A_CHUNK):
                rs = slice(n * A_CHUNK, (n + 1) * A_CHUNK)
                sp = jnp.dot(w_g, vn_ref[rs, gc], preferred_element_type=F32) + bs_ref[g]
                mixed_ref[rs, gc] = (ug_ref[rs, gc] * sp).astype(BF16)

    def out_a():
        val["y_a"] = jnp.dot(mixed_ref[:, :D_A], w_out_ref[:D_A, :],
                             preferred_element_type=F32)

    def out_b():
        val["y_b"] = jnp.dot(mixed_ref[:, D_A:], w_out_ref[D_A:, :],
                             preferred_element_type=F32)

    phases = dict(
        gate_lowrank=gate_lowrank, gate_log_decay=gate_log_decay, gla_q=gla_q, gla_k=gla_k,
        gla_cumsum=gla_cumsum, gla_v=gla_v, gla_operands=gla_operands, gla_gate=gla_gate,
        gla_state=gla_state, gla_scores=gla_scores, gla_output=gla_output,
        sgu_value=sgu_value, sgu_u=sgu_u, sgu_gate=sgu_gate, sgu_mix=sgu_mix,
        out_a=out_a, out_b=out_b)
    assert sorted(PHASE_ORDER) == sorted(phases)
    for name in PHASE_ORDER:
        phases[name]()

    y = val["y_a"] + val["y_b"]
    o_ref[...] = x_ref[...] + y * _rms_scale(y) * post_g_ref[...]


def _resident(shape):
    zeros = (0,) * len(shape)
    return pl.BlockSpec(shape, lambda b, s: zeros, pipeline_mode=pl.Buffered(1))


def _hybrid_layer(x, pre_g, w_in, w_a2, b_a2, ln_g, ln_b, w_s, b_s, bn_g, w_out, post_g):
    bsz, seq, d = x.shape
    assert d == D_MODEL and seq % SEQ_TILE == 0
    assert SEQ_TILE % A_CHUNK == 0 and SEQ_TILE % GLA_BLOCK == 0 and GLA_BLOCK % B_CHUNK == 0
    assert w_in.shape == (D_MODEL, D_IN) and w_out.shape == (D_MIX, D_MODEL)

    w_in_p = jnp.pad(w_in, ((0, 0), (0, D_IN_PAD - D_IN))).astype(BF16)
    w_a2_p = jnp.pad(w_a2, ((0, LR_PAD - B_GATE_RANK), (0, 0))).astype(BF16)
    row_vec = lambda a: a.reshape(1, -1).astype(F32)

    x_spec = pl.BlockSpec((None, SEQ_TILE, D_MODEL), lambda b, s: (b, s, 0))
    n_chunks = SEQ_TILE // B_CHUNK
    scratch = [
        pltpu.VMEM((SEQ_TILE, D_MODEL), BF16),
        pltpu.VMEM((SEQ_TILE, D_A), BF16),
        pltpu.VMEM((SEQ_TILE, D_A), F32),
        pltpu.VMEM((SEQ_TILE, D_QK), BF16),
        pltpu.VMEM((SEQ_TILE, D_QK), BF16),
        pltpu.VMEM((SEQ_TILE, D_QK), BF16),
        pltpu.VMEM((SEQ_TILE, D_QK), BF16),
        pltpu.VMEM((SEQ_TILE, D_B), BF16),
        pltpu.VMEM((SEQ_TILE, D_B), F32),
        pltpu.VMEM((SEQ_TILE, D_QK), F32),
        pltpu.VMEM((n_chunks, B_HEADS, B_DV, B_DK), BF16),
        pltpu.VMEM((SEQ_TILE, D_MIX), BF16),
        pltpu.VMEM((B_HEADS, B_DV, B_DK), F32),
    ]
    return pl.pallas_call(
        _layer_kernel,
        out_shape=jax.ShapeDtypeStruct(x.shape, x.dtype),
        grid=(bsz, seq // SEQ_TILE),
        in_specs=[
            x_spec,
            _resident((1, D_MODEL)),
            _resident((D_MODEL, D_IN_PAD)),
            _resident((LR_PAD, D_QK)),
            _resident((1, D_QK)),
            _resident((1, D_A)),
            _resident((1, D_A)),
            _resident((A_GROUPS, A_CHUNK, A_CHUNK)),
            _resident((A_GROUPS, A_CHUNK, 1)),
            _resident((1, D_B)),
            _resident((D_MIX, D_MODEL)),
            _resident((1, D_MODEL)),
        ],
        out_specs=x_spec,
        scratch_shapes=scratch,
        compiler_params=pltpu.CompilerParams(
            dimension_semantics=("arbitrary", "arbitrary"),
            vmem_limit_bytes=V7X_VMEM_LIMIT_BYTES),
        name="hybrid_gmlp_gla_layer",
    )(x, row_vec(pre_g), w_in_p, w_a2_p, row_vec(b_a2), row_vec(ln_g), row_vec(ln_b),
      w_s.astype(F32), b_s.astype(F32).reshape(A_GROUPS, A_CHUNK, 1), row_vec(bn_g),
      w_out.astype(BF16), row_vec(post_g))


def kernel(x, pre_norm_g, w_in, w_a2, b_a2, a_ln_g, a_ln_b, a_w_s, a_b_s, b_norm_g, w_out, post_norm_g):
    for l in range(pre_norm_g.shape[0]):
        x = _hybrid_layer(x, pre_norm_g[l], w_in[l], w_a2[l], b_a2[l], a_ln_g[l], a_ln_b[l],
                          a_w_s[l], a_b_s[l], b_norm_g[l], w_out[l], post_norm_g[l])
    return x
```

```python
import jax
import jax.numpy as jnp
from jax import lax
from jax.experimental import pallas as pl
from jax.experimental.pallas import tpu as pltpu

D_MODEL = 1024
D_MIX = 2 * D_MODEL
D_A = D_MIX // 2
D_B = D_MIX - D_A
A_GROUPS = 8
A_GROUP_DIM = D_A // A_GROUPS
A_CHUNK = 128
B_HEADS = 4
B_DK = D_B // 2 // B_HEADS
B_DV = D_B // B_HEADS
B_GATE_RANK = 16
B_GATE_TAU = 16.0
B_CHUNK = 64
D_QK = B_HEADS * B_DK
EPS = 1e-6

OFF_U = 0
OFF_V = OFF_U + D_A
OFF_AG = OFF_V + D_A
OFF_Q = OFF_AG + D_A
OFF_K = OFF_Q + D_QK
OFF_BV = OFF_K + D_QK
OFF_BG = OFF_BV + D_B
OFF_LR = OFF_BG + D_B
D_IN = OFF_LR + B_GATE_RANK

V7X_LANES = 128
LR_PAD = V7X_LANES

SEQ_TILE = 512
GLA_BLOCK = 256
V7X_VMEM_LIMIT_BYTES = 56 * 1024 * 1024
WEIGHT_STAGE_ROWS = 1024
WEIGHT_STAGE_COLS = 512

PHASE_ORDER = (
    "gate_lowrank", "sgu_value", "gate_log_decay", "sgu_u", "sgu_gate", "sgu_mix",
    "gla_q", "gla_k", "gla_cumsum", "out_a", "gla_v", "gla_operands", "gla_gate",
    "gla_state", "gla_scores", "gla_output", "out_b",
)

F32 = jnp.float32
BF16 = jnp.bfloat16


def _gelu(x):
    return 0.5 * x * (1.0 + lax.erf(x * (0.5 ** 0.5)))


def _silu(x):
    half = 0.5 * x
    return half + half * jnp.tanh(half)


def _log_sigmoid(x):
    return jnp.minimum(x, 0.0) - jnp.log(1.0 + jnp.exp(-jnp.abs(x)))


def _rms_scale(x):
    return lax.rsqrt(jnp.mean(x * x, axis=-1, keepdims=True) + EPS)


def _load_weights_as_bf16(w_in_hbm, w_out_hbm, w_in_ref, w_out_ref, stage_ref, sem):
    rows, cols = stage_ref.shape[1:]
    jobs = [(w_in_hbm, w_in_ref, 0, c0) for c0 in range(0, w_in_ref.shape[1], cols)]
    jobs += [(w_out_hbm, w_out_ref, r0, c0)
             for r0 in range(0, w_out_ref.shape[0], rows)
             for c0 in range(0, w_out_ref.shape[1], cols)]

    def copy(i):
        src, _, r0, c0 = jobs[i]
        return pltpu.make_async_copy(src.at[pl.ds(r0, rows), pl.ds(c0, cols)],
                                     stage_ref.at[i % 2], sem.at[i % 2])

    copy(0).start()
    for i, (_, dst, r0, c0) in enumerate(jobs):
        if i + 1 < len(jobs):
            copy(i + 1).start()
        copy(i).wait()
        dst[r0:r0 + rows, c0:c0 + cols] = stage_ref[i % 2].astype(BF16)


def _layer_kernel(x_ref, pre_g_ref, w_in_hbm, w_lr_ref, w_a2_ref, b_a2_ref, ln_g_ref, ln_b_ref,
                  ws_ref, bs_ref, bn_g_ref, w_out_hbm, post_g_ref, o_ref,
                  w_in_ref, w_out_ref, stage_ref, stage_sem,
                  h_ref, vn_ref, ug_ref, qe_ref, ke_ref, kd_ref, qb_ref, v_ref, bg_ref,
                  bc_ref, sprev_ref, mixed_ref, state_ref):
    tile = x_ref.shape[0]
    n_chunks = tile // B_CHUNK
    blocks = [slice(i * GLA_BLOCK, (i + 1) * GLA_BLOCK) for i in range(tile // GLA_BLOCK)]
    nt = (((1,), (1,)), ((), ()))
    tn = (((0,), (0,)), ((), ()))

    @pl.when((pl.program_id(0) == 0) & (pl.program_id(1) == 0))
    def _first_step():
        _load_weights_as_bf16(w_in_hbm, w_out_hbm, w_in_ref, w_out_ref, stage_ref, stage_sem)

    @pl.when(pl.program_id(1) == 0)
    def _reset_state():
        state_ref[...] = jnp.zeros_like(state_ref)

    x = x_ref[...]
    h_ref[...] = (x * _rms_scale(x) * pre_g_ref[...]).astype(BF16)

    def proj(off, width):
        return jnp.dot(h_ref[...], w_in_ref[:, off:off + width],
                       preferred_element_type=F32)

    val = {}

    def gate_lowrank():
        val["lr"] = jnp.dot(h_ref[...], w_lr_ref[...],
                            preferred_element_type=F32).astype(BF16)

    def gate_log_decay():
        gate_logit = (jnp.dot(val["lr"], w_a2_ref[...], preferred_element_type=F32)
                      + b_a2_ref[...])
        val["log_a"] = _log_sigmoid(gate_logit) * (1.0 / B_GATE_TAU)

    def gla_q():
        val["q"] = proj(OFF_Q, D_QK) * (B_DK ** -0.5)

    def gla_k():
        val["k"] = proj(OFF_K, D_QK)

    r_t = lax.broadcasted_iota(jnp.int32, (GLA_BLOCK, GLA_BLOCK), 0)
    c_t = lax.broadcasted_iota(jnp.int32, (GLA_BLOCK, GLA_BLOCK), 1)
    shift = B_CHUNK.bit_length() - 1
    same_chunk_causal = (r_t >= c_t) & ((r_t >> shift) == (c_t >> shift))

    def gla_cumsum():
        cum_mat = same_chunk_causal.astype(BF16)
        log_a = val["log_a"]
        la_hi = log_a.astype(BF16)
        la_lo = (log_a - la_hi.astype(F32)).astype(BF16)
        for blk in blocks:
            bc_ref[blk, :] = (jnp.dot(cum_mat, la_hi[blk], preferred_element_type=F32)
                              + jnp.dot(cum_mat, la_lo[blk], preferred_element_type=F32))

    def gla_v():
        v_ref[...] = proj(OFF_BV, D_B).astype(BF16)

    def gla_operands():
        q, k = val["q"], val["k"]
        decays = []
        for c in range(n_chunks):
            r0 = c * B_CHUNK
            rs = slice(r0, r0 + B_CHUNK)
            b = bc_ref[rs, :]
            b_last = bc_ref[r0 + B_CHUNK - 1:r0 + B_CHUNK, :]
            b_mid = bc_ref[r0 + B_CHUNK // 2 - 1:r0 + B_CHUNK // 2, :]
            q_c = q[rs]
            k_c = k[rs]
            qe_ref[rs, :] = (q_c * jnp.exp(b - b_mid)).astype(BF16)
            ke_ref[rs, :] = (k_c * jnp.exp(b_mid - b)).astype(BF16)
            kd_ref[rs, :] = (k_c * jnp.exp(b_last - b)).astype(BF16)
            qb_ref[rs, :] = (q_c * jnp.exp(b)).astype(BF16)
            decays.append(jnp.exp(b_last))
        val["decays"] = decays

    def gla_gate():
        bg_ref[...] = _silu(proj(OFF_BG, D_B))

    def gla_state():
        for hd in range(B_HEADS):
            kc_ = slice(hd * B_DK, (hd + 1) * B_DK)
            vc_ = slice(hd * B_DV, (hd + 1) * B_DV)
            s_t = state_ref[hd]
            for c in range(n_chunks):
                rs = slice(c * B_CHUNK, (c + 1) * B_CHUNK)
                sprev_ref[c, hd] = s_t.astype(BF16)
                kv_t = lax.dot_general(v_ref[rs, vc_], kd_ref[rs, kc_], tn,
                                       preferred_element_type=F32)
                s_t = s_t * val["decays"][c][:, kc_] + kv_t
            state_ref[hd] = s_t

    def gla_scores():
        scores = {}
        for bi, blk in enumerate(blocks):
            for hd in range(B_HEADS):
                kc_ = slice(hd * B_DK, (hd + 1) * B_DK)
                s_hd = lax.dot_general(qe_ref[blk, kc_], ke_ref[blk, kc_], nt,
                                       preferred_element_type=F32)
                scores[bi, hd] = jnp.where(same_chunk_causal, s_hd, 0.0).astype(BF16)
        val["scores"] = scores

    def gla_output():
        chunks_per_block = GLA_BLOCK // B_CHUNK
        for bi, blk in enumerate(blocks):
            for hd in range(B_HEADS):
                kc_ = slice(hd * B_DK, (hd + 1) * B_DK)
                vc_ = slice(hd * B_DV, (hd + 1) * B_DV)
                o_intra = jnp.dot(val["scores"][bi, hd], v_ref[blk, vc_],
                                  preferred_element_type=F32)
                o_inter = jnp.concatenate([
                    lax.dot_general(qb_ref[c * B_CHUNK:(c + 1) * B_CHUNK, kc_],
                                    sprev_ref[c, hd], nt, preferred_element_type=F32)
                    for c in range(bi * chunks_per_block, (bi + 1) * chunks_per_block)], axis=0)
                o = o_intra + o_inter
                o_n = o * _rms_scale(o) * bn_g_ref[:, vc_]
                mixed_ref[blk, D_A + hd * B_DV:D_A + (hd + 1) * B_DV] = (
                    o_n * bg_ref[blk, vc_]).astype(BF16)

    def sgu_value():
        v_a = _gelu(proj(OFF_V, D_A))
        mu = jnp.mean(v_a, axis=-1, keepdims=True)
        v_c = v_a - mu
        v_n = v_c * lax.rsqrt(jnp.mean(v_c * v_c, axis=-1, keepdims=True) + EPS)
        vn_ref[...] = (v_n * ln_g_ref[...] + ln_b_ref[...]).astype(BF16)

    def sgu_u():
        val["u_a"] = _gelu(proj(OFF_U, D_A))

    def sgu_gate():
        ug_ref[...] = val["u_a"] * _silu(proj(OFF_AG, D_A))

    def sgu_mix():
        row = lax.broadcasted_iota(jnp.int32, (A_CHUNK, A_CHUNK), 0)
        col = lax.broadcasted_iota(jnp.int32, (A_CHUNK, A_CHUNK), 1)
        causal_a = row >= col
        for g in range(A_GROUPS):
            w_g = jnp.where(causal_a, ws_ref[g], 0.0).astype(BF16)
            gc = slice(g * A_GROUP_DIM, (g + 1) * A_GROUP_DIM)
            for n in range(tile // A_CHUNK):
                rs = slice(n * A_CHUNK, (n + 1) * A_CHUNK)
                sp = jnp.dot(w_g, vn_ref[rs, gc], preferred_element_type=F32) + bs_ref[g]
                mixed_ref[rs, gc] = (ug_ref[rs, gc] * sp).astype(BF16)

    def out_a():
        val["y_a"] = jnp.dot(mixed_ref[:, :D_A], w_out_ref[:D_A, :],
                             preferred_element_type=F32)

    def out_b():
        val["y_b"] = jnp.dot(mixed_ref[:, D_A:], w_out_ref[D_A:, :],
                             preferred_element_type=F32)

    phases = dict(
        gate_lowrank=gate_lowrank, gate_log_decay=gate_log_decay, gla_q=gla_q, gla_k=gla_k,
        gla_cumsum=gla_cumsum, gla_v=gla_v, gla_operands=gla_operands, gla_gate=gla_gate,
        gla_state=gla_state, gla_scores=gla_scores, gla_output=gla_output,
        sgu_value=sgu_value, sgu_u=sgu_u, sgu_gate=sgu_gate, sgu_mix=sgu_mix,
        out_a=out_a, out_b=out_b)
    assert sorted(PHASE_ORDER) == sorted(phases)
    for name in PHASE_ORDER:
        phases[name]()

    y = val["y_a"] + val["y_b"]
    o_ref[...] = x_ref[...] + y * _rms_scale(y) * post_g_ref[...]


def _resident(shape):
    zeros = (0,) * len(shape)
    return pl.BlockSpec(shape, lambda b, s: zeros, pipeline_mode=pl.Buffered(1))


def _hybrid_layer(x, pre_g, w_in, w_a2, b_a2, ln_g, ln_b, w_s, b_s, bn_g, w_out, post_g):
    bsz, seq, d = x.shape
    assert d == D_MODEL and seq % SEQ_TILE == 0
    assert SEQ_TILE % A_CHUNK == 0 and SEQ_TILE % GLA_BLOCK == 0 and GLA_BLOCK % B_CHUNK == 0
    assert w_in.shape == (D_MODEL, D_IN) and w_out.shape == (D_MIX, D_MODEL)

    w_lr_p = jnp.pad(w_in[:, OFF_LR:], ((0, 0), (0, LR_PAD - B_GATE_RANK))).astype(BF16)
    w_a2_p = jnp.pad(w_a2, ((0, LR_PAD - B_GATE_RANK), (0, 0))).astype(BF16)
    row_vec = lambda a: a.reshape(1, -1).astype(F32)

    x_spec = pl.BlockSpec((None, SEQ_TILE, D_MODEL), lambda b, s: (b, s, 0))
    hbm_spec = pl.BlockSpec(memory_space=pl.ANY)
    n_chunks = SEQ_TILE // B_CHUNK
    scratch = [
        pltpu.VMEM((D_MODEL, OFF_LR), BF16),
        pltpu.VMEM((D_MIX, D_MODEL), BF16),
        pltpu.VMEM((2, WEIGHT_STAGE_ROWS, WEIGHT_STAGE_COLS), F32),
        pltpu.SemaphoreType.DMA((2,)),
        pltpu.VMEM((SEQ_TILE, D_MODEL), BF16),
        pltpu.VMEM((SEQ_TILE, D_A), BF16),
        pltpu.VMEM((SEQ_TILE, D_A), F32),
        pltpu.VMEM((SEQ_TILE, D_QK), BF16),
        pltpu.VMEM((SEQ_TILE, D_QK), BF16),
        pltpu.VMEM((SEQ_TILE, D_QK), BF16),
        pltpu.VMEM((SEQ_TILE, D_QK), BF16),
        pltpu.VMEM((SEQ_TILE, D_B), BF16),
        pltpu.VMEM((SEQ_TILE, D_B), F32),
        pltpu.VMEM((SEQ_TILE, D_QK), F32),
        pltpu.VMEM((n_chunks, B_HEADS, B_DV, B_DK), BF16),
        pltpu.VMEM((SEQ_TILE, D_MIX), BF16),
        pltpu.VMEM((B_HEADS, B_DV, B_DK), F32),
    ]
    return pl.pallas_call(
        _layer_kernel,
        out_shape=jax.ShapeDtypeStruct(x.shape, x.dtype),
        grid=(bsz, seq // SEQ_TILE),
        in_specs=[
            x_spec,
            _resident((1, D_MODEL)),
            hbm_spec,
            _resident((D_MODEL, LR_PAD)),
            _resident((LR_PAD, D_QK)),
            _resident((1, D_QK)),
            _resident((1, D_A)),
            _resident((1, D_A)),
            _resident((A_GROUPS, A_CHUNK, A_CHUNK)),
            _resident((A_GROUPS, A_CHUNK, 1)),
            _resident((1, D_B)),
            hbm_spec,
            _resident((1, D_MODEL)),
        ],
        out_specs=x_spec,
        scratch_shapes=scratch,
        compiler_params=pltpu.CompilerParams(
            dimension_semantics=("arbitrary", "arbitrary"),
            vmem_limit_bytes=V7X_VMEM_LIMIT_BYTES),
        name="hybrid_gmlp_gla_layer",
    )(x, row_vec(pre_g), w_in.astype(F32), w_lr_p, w_a2_p, row_vec(b_a2), row_vec(ln_g),
      row_vec(ln_b), w_s.astype(F32), b_s.astype(F32).reshape(A_GROUPS, A_CHUNK, 1),
      row_vec(bn_g), w_out.astype(F32), row_vec(post_g))


def kernel(x, pre_norm_g, w_in, w_a2, b_a2, a_ln_g, a_ln_b, a_w_s, a_b_s, b_norm_g, w_out, post_norm_g):
    for l in range(pre_norm_g.shape[0]):
        x = _hybrid_layer(x, pre_norm_g[l], w_in[l], w_a2[l], b_a2[l], a_ln_g[l], a_ln_b[l],
                          a_w_s[l], a_b_s[l], b_norm_g[l], w_out[l], post_norm_g[l])
    return x
```

```python
import jax
import jax.numpy as jnp
from jax import lax
from jax.experimental import pallas as pl
from jax.experimental.pallas import tpu as pltpu

D_MODEL = 1024
D_MIX = 2 * D_MODEL
D_A = D_MIX // 2
D_B = D_MIX - D_A
A_GROUPS = 8
A_GROUP_DIM = D_A // A_GROUPS
A_CHUNK = 128
B_HEADS = 4
B_DK = D_B // 2 // B_HEADS
B_DV = D_B // B_HEADS
B_GATE_RANK = 16
B_GATE_TAU = 16.0
B_CHUNK = 64
D_QK = B_HEADS * B_DK
EPS = 1e-6

OFF_U = 0
OFF_V = OFF_U + D_A
OFF_AG = OFF_V + D_A
OFF_Q = OFF_AG + D_A
OFF_K = OFF_Q + D_QK
OFF_BV = OFF_K + D_QK
OFF_BG = OFF_BV + D_B
OFF_LR = OFF_BG + D_B
D_IN = OFF_LR + B_GATE_RANK

V7X_LANES = 128
LR_PAD = V7X_LANES

SEQ_TILE = 512
GLA_BLOCK = 256
V7X_VMEM_LIMIT_BYTES = 56 * 1024 * 1024
WEIGHT_STAGE_ROWS = 512

PHASE_ORDER = (
    "gate_lowrank", "sgu_value", "gate_log_decay", "sgu_u", "sgu_gate", "sgu_mix",
    "gla_q", "gla_k", "gla_cumsum", "out_a", "gla_v", "gla_operands", "gla_gate",
    "gla_state", "gla_scores", "gla_output", "out_b",
)

F32 = jnp.float32
BF16 = jnp.bfloat16


def _gelu(x):
    return 0.5 * x * (1.0 + lax.erf(x * (0.5 ** 0.5)))


def _silu(x):
    half = 0.5 * x
    return half + half * jnp.tanh(half)


def _log_sigmoid(x):
    return jnp.minimum(x, 0.0) - jnp.log(1.0 + jnp.exp(-jnp.abs(x)))


def _rms_scale(x):
    return lax.rsqrt(jnp.mean(x * x, axis=-1, keepdims=True) + EPS)


def _load_weights_as_bf16(w_in_t_hbm, w_out_hbm, w_in_ref, w_out_ref, stage_ref, sem):
    rows = stage_ref.shape[1]
    jobs = [(w_in_t_hbm, r0, True) for r0 in range(0, w_in_ref.shape[1], rows)]
    jobs += [(w_out_hbm, r0, False) for r0 in range(0, w_out_ref.shape[0], rows)]

    def copy(i):
        src, r0, _ = jobs[i]
        return pltpu.make_async_copy(src.at[pl.ds(r0, rows), :],
                                     stage_ref.at[i % 2], sem.at[i % 2])

    copy(0).start()
    for i, (_, r0, is_w_in) in enumerate(jobs):
        if i + 1 < len(jobs):
            copy(i + 1).start()
        copy(i).wait()
        if is_w_in:
            w_in_ref[:, r0:r0 + rows] = stage_ref[i % 2].T.astype(BF16)
        else:
            w_out_ref[r0:r0 + rows, :] = stage_ref[i % 2].astype(BF16)


def _layer_kernel(x_ref, pre_g_ref, w_in_hbm, w_lr_t_ref, w_a2_ref, b_a2_ref, ln_g_ref, ln_b_ref,
                  ws_ref, bs_ref, bn_g_ref, w_out_hbm, post_g_ref, o_ref,
                  w_in_ref, w_out_ref, stage_ref, stage_sem,
                  h_ref, vn_ref, ug_ref, qe_ref, ke_ref, kd_ref, qb_ref, v_ref, bg_ref,
                  bc_ref, sprev_ref, mixed_ref, state_ref):
    tile = x_ref.shape[0]
    n_chunks = tile // B_CHUNK
    blocks = [slice(i * GLA_BLOCK, (i + 1) * GLA_BLOCK) for i in range(tile // GLA_BLOCK)]
    nt = (((1,), (1,)), ((), ()))
    tn = (((0,), (0,)), ((), ()))

    @pl.when((pl.program_id(0) == 0) & (pl.program_id(1) == 0))
    def _first_step():
        _load_weights_as_bf16(w_in_hbm, w_out_hbm, w_in_ref, w_out_ref, stage_ref, stage_sem)

    @pl.when(pl.program_id(1) == 0)
    def _reset_state():
        state_ref[...] = jnp.zeros_like(state_ref)

    x = x_ref[...]
    h_ref[...] = (x * _rms_scale(x) * pre_g_ref[...]).astype(BF16)

    def proj(off, width):
        return jnp.dot(h_ref[...], w_in_ref[:, off:off + width],
                       preferred_element_type=F32)

    val = {}

    def gate_lowrank():
        val["lr"] = lax.dot_general(h_ref[...], w_lr_t_ref[...], nt,
                                    preferred_element_type=F32).astype(BF16)

    def gate_log_decay():
        gate_logit = (jnp.dot(val["lr"], w_a2_ref[...], preferred_element_type=F32)
                      + b_a2_ref[...])
        val["log_a"] = _log_sigmoid(gate_logit) * (1.0 / B_GATE_TAU)

    def gla_q():
        val["q"] = proj(OFF_Q, D_QK) * (B_DK ** -0.5)

    def gla_k():
        val["k"] = proj(OFF_K, D_QK)

    r_t = lax.broadcasted_iota(jnp.int32, (GLA_BLOCK, GLA_BLOCK), 0)
    c_t = lax.broadcasted_iota(jnp.int32, (GLA_BLOCK, GLA_BLOCK), 1)
    shift = B_CHUNK.bit_length() - 1
    same_chunk_causal = (r_t >= c_t) & ((r_t >> shift) == (c_t >> shift))

    def gla_cumsum():
        cum_mat = same_chunk_causal.astype(BF16)
        log_a = val["log_a"]
        la_hi = log_a.astype(BF16)
        la_lo = (log_a - la_hi.astype(F32)).astype(BF16)
        for blk in blocks:
            bc_ref[blk, :] = (jnp.dot(cum_mat, la_hi[blk], preferred_element_type=F32)
                              + jnp.dot(cum_mat, la_lo[blk], preferred_element_type=F32))

    def gla_v():
        v_ref[...] = proj(OFF_BV, D_B).astype(BF16)

    def gla_operands():
        q, k = val["q"], val["k"]
        decays = []
        for c in range(n_chunks):
            r0 = c * B_CHUNK
            rs = slice(r0, r0 + B_CHUNK)
            b = bc_ref[rs, :]
            b_last = bc_ref[r0 + B_CHUNK - 1:r0 + B_CHUNK, :]
            b_mid = bc_ref[r0 + B_CHUNK // 2 - 1:r0 + B_CHUNK // 2, :]
            q_c = q[rs]
            k_c = k[rs]
            qe_ref[rs, :] = (q_c * jnp.exp(b - b_mid)).astype(BF16)
            ke_ref[rs, :] = (k_c * jnp.exp(b_mid - b)).astype(BF16)
            kd_ref[rs, :] = (k_c * jnp.exp(b_last - b)).astype(BF16)
            qb_ref[rs, :] = (q_c * jnp.exp(b)).astype(BF16)
            decays.append(jnp.exp(b_last))
        val["decays"] = decays

    def gla_gate():
        bg_ref[...] = _silu(proj(OFF_BG, D_B))

    def gla_state():
        for hd in range(B_HEADS):
            kc_ = slice(hd * B_DK, (hd + 1) * B_DK)
            vc_ = slice(hd * B_DV, (hd + 1) * B_DV)
            s_t = state_ref[hd]
            for c in range(n_chunks):
                rs = slice(c * B_CHUNK, (c + 1) * B_CHUNK)
                sprev_ref[c, hd] = s_t.astype(BF16)
                kv_t = lax.dot_general(v_ref[rs, vc_], kd_ref[rs, kc_], tn,
                                       preferred_element_type=F32)
                s_t = s_t * val["decays"][c][:, kc_] + kv_t
            state_ref[hd] = s_t

    def gla_scores():
        scores = {}
        for bi, blk in enumerate(blocks):
            for hd in range(B_HEADS):
                kc_ = slice(hd * B_DK, (hd + 1) * B_DK)
                s_hd = lax.dot_general(qe_ref[blk, kc_], ke_ref[blk, kc_], nt,
                                       preferred_element_type=F32)
                scores[bi, hd] = jnp.where(same_chunk_causal, s_hd, 0.0).astype(BF16)
        val["scores"] = scores

    def gla_output():
        chunks_per_block = GLA_BLOCK // B_CHUNK
        for bi, blk in enumerate(blocks):
            for hd in range(B_HEADS):
                kc_ = slice(hd * B_DK, (hd + 1) * B_DK)
                vc_ = slice(hd * B_DV, (hd + 1) * B_DV)
                o_intra = jnp.dot(val["scores"][bi, hd], v_ref[blk, vc_],
                                  preferred_element_type=F32)
                o_inter = jnp.concatenate([
                    lax.dot_general(qb_ref[c * B_CHUNK:(c + 1) * B_CHUNK, kc_],
                                    sprev_ref[c, hd], nt, preferred_element_type=F32)
                    for c in range(bi * chunks_per_block, (bi + 1) * chunks_per_block)], axis=0)
                o = o_intra + o_inter
                o_n = o * _rms_scale(o) * bn_g_ref[:, vc_]
                mixed_ref[blk, D_A + hd * B_DV:D_A + (hd + 1) * B_DV] = (
                    o_n * bg_ref[blk, vc_]).astype(BF16)

    def sgu_value():
        v_a = _gelu(proj(OFF_V, D_A))
        mu = jnp.mean(v_a, axis=-1, keepdims=True)
        v_c = v_a - mu
        v_n = v_c * lax.rsqrt(jnp.mean(v_c * v_c, axis=-1, keepdims=True) + EPS)
        vn_ref[...] = (v_n * ln_g_ref[...] + ln_b_ref[...]).astype(BF16)

    def sgu_u():
        val["u_a"] = _gelu(proj(OFF_U, D_A))

    def sgu_gate():
        ug_ref[...] = val["u_a"] * _silu(proj(OFF_AG, D_A))

    def sgu_mix():
        row = lax.broadcasted_iota(jnp.int32, (A_CHUNK, A_CHUNK), 0)
        col = lax.broadcasted_iota(jnp.int32, (A_CHUNK, A_CHUNK), 1)
        causal_a = row >= col
        for g in range(A_GROUPS):
            w_g = jnp.where(causal_a, ws_ref[g], 0.0).astype(BF16)
            gc = slice(g * A_GROUP_DIM, (g + 1) * A_GROUP_DIM)
            for n in range(tile // A_CHUNK):
                rs = slice(n * A_CHUNK, (n + 1) * A_CHUNK)
                sp = jnp.dot(w_g, vn_ref[rs, gc], preferred_element_type=F32) + bs_ref[g]
                mixed_ref[rs, gc] = (ug_ref[rs, gc] * sp).astype(BF16)

    def out_a():
        val["y_a"] = jnp.dot(mixed_ref[:, :D_A], w_out_ref[:D_A, :],
                             preferred_element_type=F32)

    def out_b():
        val["y_b"] = jnp.dot(mixed_ref[:, D_A:], w_out_ref[D_A:, :],
                             preferred_element_type=F32)

    phases = dict(
        gate_lowrank=gate_lowrank, gate_log_decay=gate_log_decay, gla_q=gla_q, gla_k=gla_k,
        gla_cumsum=gla_cumsum, gla_v=gla_v, gla_operands=gla_operands, gla_gate=gla_gate,
        gla_state=gla_state, gla_scores=gla_scores, gla_output=gla_output,
        sgu_value=sgu_value, sgu_u=sgu_u, sgu_gate=sgu_gate, sgu_mix=sgu_mix,
        out_a=out_a, out_b=out_b)
    assert sorted(PHASE_ORDER) == sorted(phases)
    for name in PHASE_ORDER:
        phases[name]()

    y = val["y_a"] + val["y_b"]
    o_ref[...] = x_ref[...] + y * _rms_scale(y) * post_g_ref[...]


def _resident(shape):
    zeros = (0,) * len(shape)
    return pl.BlockSpec(shape, lambda b, s: zeros, pipeline_mode=pl.Buffered(1))


def _hybrid_layer(x, pre_g, w_in, w_a2, b_a2, ln_g, ln_b, w_s, b_s, bn_g, w_out, post_g):
    bsz, seq, d = x.shape
    assert d == D_MODEL and seq % SEQ_TILE == 0
    assert SEQ_TILE % A_CHUNK == 0 and SEQ_TILE % GLA_BLOCK == 0 and GLA_BLOCK % B_CHUNK == 0
    assert w_in.shape == (D_MODEL, D_IN) and w_out.shape == (D_MIX, D_MODEL)

    w_in_t = jnp.swapaxes(w_in, 0, 1).astype(F32)
    w_lr_t_p = jnp.pad(w_in_t[OFF_LR:], ((0, LR_PAD - B_GATE_RANK), (0, 0))).astype(BF16)
    w_a2_p = jnp.pad(w_a2, ((0, LR_PAD - B_GATE_RANK), (0, 0))).astype(BF16)
    row_vec = lambda a: a.reshape(1, -1).astype(F32)

    x_spec = pl.BlockSpec((None, SEQ_TILE, D_MODEL), lambda b, s: (b, s, 0))
    hbm_spec = pl.BlockSpec(memory_space=pl.ANY)
    n_chunks = SEQ_TILE // B_CHUNK
    scratch = [
        pltpu.VMEM((D_MODEL, OFF_LR), BF16),
        pltpu.VMEM((D_MIX, D_MODEL), BF16),
        pltpu.VMEM((2, WEIGHT_STAGE_ROWS, D_MODEL), F32),
        pltpu.SemaphoreType.DMA((2,)),
        pltpu.VMEM((SEQ_TILE, D_MODEL), BF16),
        pltpu.VMEM((SEQ_TILE, D_A), BF16),
        pltpu.VMEM((SEQ_TILE, D_A), F32),
        pltpu.VMEM((SEQ_TILE, D_QK), BF16),
        pltpu.VMEM((SEQ_TILE, D_QK), BF16),
        pltpu.VMEM((SEQ_TILE, D_QK), BF16),
        pltpu.VMEM((SEQ_TILE, D_QK), BF16),
        pltpu.VMEM((SEQ_TILE, D_B), BF16),
        pltpu.VMEM((SEQ_TILE, D_B), F32),
        pltpu.VMEM((SEQ_TILE, D_QK), F32),
        pltpu.VMEM((n_chunks, B_HEADS, B_DV, B_DK), BF16),
        pltpu.VMEM((SEQ_TILE, D_MIX), BF16),
        pltpu.VMEM((B_HEADS, B_DV, B_DK), F32),
    ]
    return pl.pallas_call(
        _layer_kernel,
        out_shape=jax.ShapeDtypeStruct(x.shape, x.dtype),
        grid=(bsz, seq // SEQ_TILE),
        in_specs=[
            x_spec,
            _resident((1, D_MODEL)),
            hbm_spec,
            _resident((LR_PAD, D_MODEL)),
            _resident((LR_PAD, D_QK)),
            _resident((1, D_QK)),
            _resident((1, D_A)),
            _resident((1, D_A)),
            _resident((A_GROUPS, A_CHUNK, A_CHUNK)),
            _resident((A_GROUPS, A_CHUNK, 1)),
            _resident((1, D_B)),
            hbm_spec,
            _resident((1, D_MODEL)),
        ],
        out_specs=x_spec,
        scratch_shapes=scratch,
        compiler_params=pltpu.CompilerParams(
            dimension_semantics=("arbitrary", "arbitrary"),
            vmem_limit_bytes=V7X_VMEM_LIMIT_BYTES),
        name="hybrid_gmlp_gla_layer",
    )(x, row_vec(pre_g), w_in_t, w_lr_t_p, w_a2_p, row_vec(b_a2), row_vec(ln_g),
      row_vec(ln_b), w_s.astype(F32), b_s.astype(F32).reshape(A_GROUPS, A_CHUNK, 1),
      row_vec(bn_g), w_out.astype(F32), row_vec(post_g))


def kernel(x, pre_norm_g, w_in, w_a2, b_a2, a_ln_g, a_ln_b, a_w_s, a_b_s, b_norm_g, w_out, post_norm_g):
    for l in range(pre_norm_g.shape[0]):
        x = _hybrid_layer(x, pre_norm_g[l], w_in[l], w_a2[l], b_a2[l], a_ln_g[l], a_ln_b[l],
                          a_w_s[l], a_b_s[l], b_norm_g[l], w_out[l], post_norm_g[l])
    return x
```

```python
import functools

import jax
import jax.numpy as jnp
from jax import lax
from jax.experimental import pallas as pl
from jax.experimental.pallas import tpu as pltpu

D_MODEL = 1024
D_MIX = 2 * D_MODEL
D_A = D_MIX // 2
D_B = D_MIX - D_A
A_GROUPS = 8
A_GROUP_DIM = D_A // A_GROUPS
A_CHUNK = 128
B_HEADS = 4
B_DK = D_B // 2 // B_HEADS
B_DV = D_B // B_HEADS
B_GATE_RANK = 16
B_GATE_TAU = 16.0
B_CHUNK = 64
D_QK = B_HEADS * B_DK
EPS = 1e-6

OFF_U = 0
OFF_V = OFF_U + D_A
OFF_AG = OFF_V + D_A
OFF_Q = OFF_AG + D_A
OFF_K = OFF_Q + D_QK
OFF_BV = OFF_K + D_QK
OFF_BG = OFF_BV + D_B
OFF_LR = OFF_BG + D_B
D_IN = OFF_LR + B_GATE_RANK

V7X_LANES = 128
LR_PAD = V7X_LANES

SEQ_TILE = 512
GLA_BLOCK = 256
V7X_VMEM_LIMIT_BYTES = 58 * 1024 * 1024
WEIGHT_STAGE_ROWS = 512

PHASE_ORDER = (
    "post_norm_prev", "sgu_value", "gate_log_decay", "sgu_u", "sgu_gate", "sgu_mix",
    "gla_q", "gla_k", "gla_cumsum", "out_a", "gla_v", "gla_operands", "gla_gate",
    "gla_state", "gla_scores", "gla_output", "pre_norm_next", "out_b", "gate_lowrank_next",
)

F32 = jnp.float32
BF16 = jnp.bfloat16


def _gelu(x):
    return 0.5 * x * (1.0 + lax.erf(x * (0.5 ** 0.5)))


def _silu(x):
    half = 0.5 * x
    return half + half * jnp.tanh(half)


def _log_sigmoid(x):
    return jnp.minimum(x, 0.0) - jnp.log(1.0 + jnp.exp(-jnp.abs(x)))


def _rms_scale(x):
    return lax.rsqrt(jnp.mean(x * x, axis=-1, keepdims=True) + EPS)


def _load_weights_as_bf16(w_in_t_hbm, w_out_hbm, w_in_ref, w_out_ref, stage_ref, sem):
    rows = stage_ref.shape[1]
    jobs = [(w_in_t_hbm, r0, True) for r0 in range(0, w_in_ref.shape[1], rows)]
    jobs += [(w_out_hbm, r0, False) for r0 in range(0, w_out_ref.shape[0], rows)]

    def copy(i):
        src, r0, _ = jobs[i]
        return pltpu.make_async_copy(src.at[pl.ds(r0, rows), :],
                                     stage_ref.at[i % 2], sem.at[i % 2])

    copy(0).start()
    for i, (_, r0, is_w_in) in enumerate(jobs):
        if i + 1 < len(jobs):
            copy(i + 1).start()
        copy(i).wait()
        if is_w_in:
            w_in_ref[:, r0:r0 + rows] = stage_ref[i % 2].T.astype(BF16)
        else:
            w_out_ref[r0:r0 + rows, :] = stage_ref[i % 2].astype(BF16)


def _layer_kernel(x_prev_ref, x_next_ref, pre_g_ref, w_in_hbm, w_lr_t_ref, w_a2_ref, b_a2_ref,
                  ln_g_ref, ln_b_ref, ws_ref, bs_ref, bn_g_ref, w_out_hbm, post_g_ref, o_ref,
                  w_in_ref, w_out_ref, stage_ref, stage_sem,
                  h_ref, lr_ref, y_ref, vn_ref, ug_ref, qe_ref, ke_ref, kd_ref, qb_ref, v_ref, bg_ref,
                  bc_ref, sprev_ref, mixed_ref, state_ref, *, n_tiles, tiles_per_seq):
    tile = h_ref.shape[0]
    n_chunks = tile // B_CHUNK
    blocks = [slice(r0, r0 + GLA_BLOCK) for r0 in range(0, tile, GLA_BLOCK)]
    nt = (((1,), (1,)), ((), ()))
    tn = (((0,), (0,)), ((), ()))
    t = pl.program_id(0)

    def pre_norm(x):
        return (x * _rms_scale(x) * pre_g_ref[...]).astype(BF16)

    def gate_lowrank_next():
        lr_ref[...] = lax.dot_general(h_ref[...], w_lr_t_ref[...], nt,
                                      preferred_element_type=F32).astype(BF16)

    def post_norm_prev():
        y = y_ref[...]
        o_ref[...] = x_prev_ref[...] + y * _rms_scale(y) * post_g_ref[...]

    @pl.when(t == 0)
    def _first_step():
        _load_weights_as_bf16(w_in_hbm, w_out_hbm, w_in_ref, w_out_ref, stage_ref, stage_sem)
        y_ref[...] = jnp.zeros_like(y_ref)
        h_ref[...] = pre_norm(x_prev_ref[...])
        gate_lowrank_next()

    @pl.when(t % tiles_per_seq == 0)
    def _reset_state():
        state_ref[...] = jnp.zeros_like(state_ref)

    r_t = lax.broadcasted_iota(jnp.int32, (GLA_BLOCK, GLA_BLOCK), 0)
    c_t = lax.broadcasted_iota(jnp.int32, (GLA_BLOCK, GLA_BLOCK), 1)
    shift = B_CHUNK.bit_length() - 1
    same_chunk_causal = (r_t >= c_t) & ((r_t >> shift) == (c_t >> shift))

    val = {}

    def proj(off, width):
        return jnp.dot(h_ref[...], w_in_ref[:, off:off + width],
                       preferred_element_type=F32)

    def pre_norm_next():
        h_ref[...] = pre_norm(x_next_ref[...])

    def gate_log_decay():
        gate_logit = (jnp.dot(lr_ref[...], w_a2_ref[...], preferred_element_type=F32)
                      + b_a2_ref[...])
        val["log_a"] = _log_sigmoid(gate_logit) * (1.0 / B_GATE_TAU)

    def gla_q():
        val["q"] = proj(OFF_Q, D_QK) * (B_DK ** -0.5)

    def gla_k():
        val["k"] = proj(OFF_K, D_QK)

    def gla_cumsum():
        cum_mat = same_chunk_causal.astype(BF16)
        log_a = val["log_a"]
        la_hi = log_a.astype(BF16)
        la_lo = (log_a - la_hi.astype(F32)).astype(BF16)
        for blk in blocks:
            bc_ref[blk, :] = (jnp.dot(cum_mat, la_hi[blk], preferred_element_type=F32)
                              + jnp.dot(cum_mat, la_lo[blk], preferred_element_type=F32))

    def gla_v():
        v_ref[...] = proj(OFF_BV, D_B).astype(BF16)

    def gla_operands():
        q, k = val["q"], val["k"]
        decays = []
        for c in range(n_chunks):
            r0 = c * B_CHUNK
            rs = slice(r0, r0 + B_CHUNK)
            b = bc_ref[rs, :]
            b_last = bc_ref[r0 + B_CHUNK - 1:r0 + B_CHUNK, :]
            b_mid = bc_ref[r0 + B_CHUNK // 2 - 1:r0 + B_CHUNK // 2, :]
            q_c = q[rs]
            k_c = k[rs]
            qe_ref[rs, :] = (q_c * jnp.exp(b - b_mid)).astype(BF16)
            ke_ref[rs, :] = (k_c * jnp.exp(b_mid - b)).astype(BF16)
            kd_ref[rs, :] = (k_c * jnp.exp(b_last - b)).astype(BF16)
            qb_ref[rs, :] = (q_c * jnp.exp(b)).astype(BF16)
            decays.append(jnp.exp(b_last))
        val["decays"] = decays

    def gla_gate():
        bg_ref[...] = _silu(proj(OFF_BG, D_B))

    def gla_state():
        for hd in range(B_HEADS):
            kc_ = slice(hd * B_DK, (hd + 1) * B_DK)
            vc_ = slice(hd * B_DV, (hd + 1) * B_DV)
            s_t = state_ref[hd]
            for c in range(n_chunks):
                rs = slice(c * B_CHUNK, (c + 1) * B_CHUNK)
                sprev_ref[c, hd] = s_t.astype(BF16)
                kv_t = lax.dot_general(v_ref[rs, vc_], kd_ref[rs, kc_], tn,
                                       preferred_element_type=F32)
                s_t = s_t * val["decays"][c][:, kc_] + kv_t
            state_ref[hd] = s_t

    def gla_scores():
        scores = {}
        for blk in blocks:
            for hd in range(B_HEADS):
                kc_ = slice(hd * B_DK, (hd + 1) * B_DK)
                s_hd = lax.dot_general(qe_ref[blk, kc_], ke_ref[blk, kc_], nt,
                                       preferred_element_type=F32)
                scores[blk.start, hd] = jnp.where(same_chunk_causal, s_hd, 0.0).astype(BF16)
        val["scores"] = scores

    def gla_output():
        for blk in blocks:
            for hd in range(B_HEADS):
                kc_ = slice(hd * B_DK, (hd + 1) * B_DK)
                vc_ = slice(hd * B_DV, (hd + 1) * B_DV)
                o_intra = jnp.dot(val["scores"][blk.start, hd], v_ref[blk, vc_],
                                  preferred_element_type=F32)
                o_inter = jnp.concatenate([
                    lax.dot_general(qb_ref[r0:r0 + B_CHUNK, kc_], sprev_ref[r0 // B_CHUNK, hd],
                                    nt, preferred_element_type=F32)
                    for r0 in range(blk.start, blk.stop, B_CHUNK)], axis=0)
                o = o_intra + o_inter
                o_n = o * _rms_scale(o) * bn_g_ref[:, vc_]
                mixed_ref[blk, D_A + hd * B_DV:D_A + (hd + 1) * B_DV] = (
                    o_n * bg_ref[blk, vc_]).astype(BF16)

    def sgu_value():
        v_a = _gelu(proj(OFF_V, D_A))
        mu = jnp.mean(v_a, axis=-1, keepdims=True)
        v_c = v_a - mu
        v_n = v_c * lax.rsqrt(jnp.mean(v_c * v_c, axis=-1, keepdims=True) + EPS)
        vn_ref[...] = (v_n * ln_g_ref[...] + ln_b_ref[...]).astype(BF16)

    def sgu_u():
        val["u_a"] = _gelu(proj(OFF_U, D_A))

    def sgu_gate():
        ug_ref[...] = val["u_a"] * _silu(proj(OFF_AG, D_A))

    def sgu_mix():
        row = lax.broadcasted_iota(jnp.int32, (A_CHUNK, A_CHUNK), 0)
        col = lax.broadcasted_iota(jnp.int32, (A_CHUNK, A_CHUNK), 1)
        causal_a = row >= col
        for g in range(A_GROUPS):
            w_g = jnp.where(causal_a, ws_ref[g], 0.0).astype(BF16)
            gc = slice(g * A_GROUP_DIM, (g + 1) * A_GROUP_DIM)
            for n in range(tile // A_CHUNK):
                rs = slice(n * A_CHUNK, (n + 1) * A_CHUNK)
                sp = jnp.dot(w_g, vn_ref[rs, gc], preferred_element_type=F32) + bs_ref[g]
                mixed_ref[rs, gc] = (ug_ref[rs, gc] * sp).astype(BF16)

    def out_a():
        y_ref[...] = jnp.dot(mixed_ref[:, :D_A], w_out_ref[:D_A, :],
                             preferred_element_type=F32)

    def out_b():
        y_ref[...] += jnp.dot(mixed_ref[:, D_A:], w_out_ref[D_A:, :],
                              preferred_element_type=F32)

    phases = dict(
        post_norm_prev=post_norm_prev, pre_norm_next=pre_norm_next,
        gate_lowrank_next=gate_lowrank_next, gate_log_decay=gate_log_decay, gla_q=gla_q, gla_k=gla_k,
        gla_cumsum=gla_cumsum, gla_v=gla_v, gla_operands=gla_operands, gla_gate=gla_gate,
        gla_state=gla_state, gla_scores=gla_scores, gla_output=gla_output,
        sgu_value=sgu_value, sgu_u=sgu_u, sgu_gate=sgu_gate, sgu_mix=sgu_mix,
        out_a=out_a, out_b=out_b)
    assert sorted(PHASE_ORDER) == sorted(phases)

    @pl.when(t < n_tiles)
    def _tile_step():
        for name in PHASE_ORDER:
            phases[name]()

    @pl.when(t == n_tiles)
    def _drain_step():
        post_norm_prev()


def _resident(shape):
    zeros = (0,) * len(shape)
    return pl.BlockSpec(shape, lambda t: zeros, pipeline_mode=pl.Buffered(1))


def _hybrid_layer(x, pre_g, w_in, w_a2, b_a2, ln_g, ln_b, w_s, b_s, bn_g, w_out, post_g):
    bsz, seq, d = x.shape
    assert d == D_MODEL and seq % SEQ_TILE == 0
    assert SEQ_TILE % A_CHUNK == 0 and SEQ_TILE % GLA_BLOCK == 0 and GLA_BLOCK % B_CHUNK == 0
    assert w_in.shape == (D_MODEL, D_IN) and w_out.shape == (D_MIX, D_MODEL)
    tiles_per_seq = seq // SEQ_TILE
    n_tiles = bsz * tiles_per_seq

    w_in_t = jnp.swapaxes(w_in, 0, 1).astype(F32)
    w_lr_t_p = jnp.pad(w_in_t[OFF_LR:], ((0, LR_PAD - B_GATE_RANK), (0, 0))).astype(BF16)
    w_a2_p = jnp.pad(w_a2, ((0, LR_PAD - B_GATE_RANK), (0, 0))).astype(BF16)
    row_vec = lambda a: a.reshape(1, -1).astype(F32)

    def tile_spec(tile_of_step):
        def index_map(t):
            u = tile_of_step(t)
            return (u // tiles_per_seq, u % tiles_per_seq, 0)
        return pl.BlockSpec((None, SEQ_TILE, D_MODEL), index_map)

    prev_spec = tile_spec(lambda t: jnp.maximum(t - 1, 0))
    next_spec = tile_spec(lambda t: jnp.minimum(t + 1, n_tiles - 1))
    hbm_spec = pl.BlockSpec(memory_space=pl.ANY)
    n_chunks = SEQ_TILE // B_CHUNK
    scratch = [
        pltpu.VMEM((D_MODEL, OFF_LR), BF16),
        pltpu.VMEM((D_MIX, D_MODEL), BF16),
        pltpu.VMEM((2, WEIGHT_STAGE_ROWS, D_MODEL), F32),
        pltpu.SemaphoreType.DMA((2,)),
        pltpu.VMEM((SEQ_TILE, D_MODEL), BF16),
        pltpu.VMEM((SEQ_TILE, LR_PAD), BF16),
        pltpu.VMEM((SEQ_TILE, D_MODEL), F32),
        pltpu.VMEM((SEQ_TILE, D_A), BF16),
        pltpu.VMEM((SEQ_TILE, D_A), F32),
        pltpu.VMEM((SEQ_TILE, D_QK), BF16),
        pltpu.VMEM((SEQ_TILE, D_QK), BF16),
        pltpu.VMEM((SEQ_TILE, D_QK), BF16),
        pltpu.VMEM((SEQ_TILE, D_QK), BF16),
        pltpu.VMEM((SEQ_TILE, D_B), BF16),
        pltpu.VMEM((SEQ_TILE, D_B), F32),
        pltpu.VMEM((SEQ_TILE, D_QK), F32),
        pltpu.VMEM((n_chunks, B_HEADS, B_DV, B_DK), BF16),
        pltpu.VMEM((SEQ_TILE, D_MIX), BF16),
        pltpu.VMEM((B_HEADS, B_DV, B_DK), F32),
    ]
    return pl.pallas_call(
        functools.partial(_layer_kernel, n_tiles=n_tiles, tiles_per_seq=tiles_per_seq),
        out_shape=jax.ShapeDtypeStruct(x.shape, x.dtype),
        grid=(n_tiles + 1,),
        in_specs=[
            prev_spec,
            next_spec,
            _resident((1, D_MODEL)),
            hbm_spec,
            _resident((LR_PAD, D_MODEL)),
            _resident((LR_PAD, D_QK)),
            _resident((1, D_QK)),
            _resident((1, D_A)),
            _resident((1, D_A)),
            _resident((A_GROUPS, A_CHUNK, A_CHUNK)),
            _resident((A_GROUPS, A_CHUNK, 1)),
            _resident((1, D_B)),
            hbm_spec,
            _resident((1, D_MODEL)),
        ],
        out_specs=prev_spec,
        scratch_shapes=scratch,
        compiler_params=pltpu.CompilerParams(
            dimension_semantics=("arbitrary",),
            vmem_limit_bytes=V7X_VMEM_LIMIT_BYTES),
        name="hybrid_gmlp_gla_layer",
    )(x, x, row_vec(pre_g), w_in_t, w_lr_t_p, w_a2_p, row_vec(b_a2), row_vec(ln_g),
      row_vec(ln_b), w_s.astype(F32), b_s.astype(F32).reshape(A_GROUPS, A_CHUNK, 1),
      row_vec(bn_g), w_out.astype(F32), row_vec(post_g))


def kernel(x, pre_norm_g, w_in, w_a2, b_a2, a_ln_g, a_ln_b, a_w_s, a_b_s, b_norm_g, w_out, post_norm_g):
    for l in range(pre_norm_g.shape[0]):
        x = _hybrid_layer(x, pre_norm_g[l], w_in[l], w_a2[l], b_a2[l], a_ln_g[l], a_ln_b[l],
                          a_w_s[l], a_b_s[l], b_norm_g[l], w_out[l], post_norm_g[l])
    return x
```

```python
import jax
import jax.numpy as jnp
from jax import lax
from jax.experimental import pallas as pl
from jax.experimental.pallas import tpu as pltpu

D_MODEL = 1024
D_MIX = 2 * D_MODEL
D_A = D_MIX // 2
D_B = D_MIX - D_A
A_GROUPS = 8
A_GROUP_DIM = D_A // A_GROUPS
A_CHUNK = 128
B_HEADS = 4
B_DK = D_B // 2 // B_HEADS
B_DV = D_B // B_HEADS
B_GATE_RANK = 16
B_GATE_TAU = 16.0
B_CHUNK = 64
D_QK = B_HEADS * B_DK
EPS = 1e-6

OFF_U = 0
OFF_V = OFF_U + D_A
OFF_AG = OFF_V + D_A
OFF_Q = OFF_AG + D_A
OFF_K = OFF_Q + D_QK
OFF_BV = OFF_K + D_QK
OFF_BG = OFF_BV + D_B
OFF_LR = OFF_BG + D_B
D_IN = OFF_LR + B_GATE_RANK

V7X_LANES = 128
LR_PAD = V7X_LANES

SEQ_TILE = 512
GLA_BLOCK = 256
V7X_VMEM_LIMIT_BYTES = 56 * 1024 * 1024
WEIGHT_STAGE_ROWS = 256
WEIGHT_STAGE_SLOTS = 4

PHASE_ORDER = (
    "gate_lowrank", "sgu_value", "gate_log_decay", "sgu_u", "sgu_gate", "sgu_mix",
    "gla_q", "gla_k", "gla_cumsum", "out_a", "gla_v", "gla_operands", "gla_gate",
    "gla_state", "gla_scores", "gla_output", "out_b",
)

F32 = jnp.float32
BF16 = jnp.bfloat16


def _gelu(x):
    return 0.5 * x * (1.0 + lax.erf(x * (0.5 ** 0.5)))


def _silu(x):
    half = 0.5 * x
    return half + half * jnp.tanh(half)


def _log_sigmoid(x):
    return jnp.minimum(x, 0.0) - jnp.log(1.0 + jnp.exp(-jnp.abs(x)))


def _rms_scale(x):
    return lax.rsqrt(jnp.mean(x * x, axis=-1, keepdims=True) + EPS)


def _load_weights_as_bf16(w_in_t_hbm, w_out_hbm, w_in_ref, w_out_ref, stage_ref, sem):
    slots, rows = stage_ref.shape[:2]
    jobs = [(w_in_t_hbm, r0, True) for r0 in range(0, w_in_ref.shape[1], rows)]
    jobs += [(w_out_hbm, r0, False) for r0 in range(0, w_out_ref.shape[0], rows)]

    def copy(i):
        src, r0, _ = jobs[i]
        return pltpu.make_async_copy(src.at[pl.ds(r0, rows), :],
                                     stage_ref.at[i % slots], sem.at[i % slots])

    for i in range(min(slots - 1, len(jobs))):
        copy(i).start()
    for i, (_, r0, is_w_in) in enumerate(jobs):
        if i + slots - 1 < len(jobs):
            copy(i + slots - 1).start()
        copy(i).wait()
        if is_w_in:
            w_in_ref[:, r0:r0 + rows] = stage_ref[i % slots].T.astype(BF16)
        else:
            w_out_ref[r0:r0 + rows, :] = stage_ref[i % slots].astype(BF16)


def _layer_kernel(x_ref, pre_g_ref, w_in_hbm, w_lr_t_ref, w_a2_ref, b_a2_ref, ln_g_ref, ln_b_ref,
                  ws_ref, bs_ref, bn_g_ref, w_out_hbm, post_g_ref, o_ref,
                  w_in_ref, w_out_ref, stage_ref, stage_sem,
                  h_ref, vn_ref, ug_ref, qe_ref, ke_ref, kd_ref, qb_ref, v_ref, bg_ref,
                  bc_ref, sprev_ref, mixed_ref, state_ref):
    tile = x_ref.shape[0]
    n_chunks = tile // B_CHUNK
    blocks = [slice(r0, r0 + GLA_BLOCK) for r0 in range(0, tile, GLA_BLOCK)]
    nt = (((1,), (1,)), ((), ()))
    tn = (((0,), (0,)), ((), ()))

    @pl.when((pl.program_id(0) == 0) & (pl.program_id(1) == 0))
    def _first_step():
        _load_weights_as_bf16(w_in_hbm, w_out_hbm, w_in_ref, w_out_ref, stage_ref, stage_sem)

    @pl.when(pl.program_id(1) == 0)
    def _reset_state():
        state_ref[...] = jnp.zeros_like(state_ref)

    x = x_ref[...]
    h_ref[...] = (x * _rms_scale(x) * pre_g_ref[...]).astype(BF16)

    def proj(off, width):
        return jnp.dot(h_ref[...], w_in_ref[:, off:off + width],
                       preferred_element_type=F32)

    val = {}

    def gate_lowrank():
        val["lr"] = lax.dot_general(h_ref[...], w_lr_t_ref[...], nt,
                                    preferred_element_type=F32).astype(BF16)

    def gate_log_decay():
        gate_logit = (jnp.dot(val["lr"], w_a2_ref[...], preferred_element_type=F32)
                      + b_a2_ref[...])
        val["log_a"] = _log_sigmoid(gate_logit) * (1.0 / B_GATE_TAU)

    def gla_q():
        val["q"] = proj(OFF_Q, D_QK) * (B_DK ** -0.5)

    def gla_k():
        val["k"] = proj(OFF_K, D_QK)

    r_t = lax.broadcasted_iota(jnp.int32, (GLA_BLOCK, GLA_BLOCK), 0)
    c_t = lax.broadcasted_iota(jnp.int32, (GLA_BLOCK, GLA_BLOCK), 1)
    shift = B_CHUNK.bit_length() - 1
    same_chunk_causal = (r_t >= c_t) & ((r_t >> shift) == (c_t >> shift))

    def gla_cumsum():
        cum_mat = same_chunk_causal.astype(BF16)
        log_a = val["log_a"]
        la_hi = log_a.astype(BF16)
        la_lo = (log_a - la_hi.astype(F32)).astype(BF16)
        for blk in blocks:
            bc_ref[blk, :] = (jnp.dot(cum_mat, la_hi[blk], preferred_element_type=F32)
                              + jnp.dot(cum_mat, la_lo[blk], preferred_element_type=F32))

    def gla_v():
        v_ref[...] = proj(OFF_BV, D_B).astype(BF16)

    def gla_operands():
        q, k = val["q"], val["k"]
        decays = []
        for c in range(n_chunks):
            r0 = c * B_CHUNK
            rs = slice(r0, r0 + B_CHUNK)
            b = bc_ref[rs, :]
            b_last = bc_ref[r0 + B_CHUNK - 1:r0 + B_CHUNK, :]
            b_mid = bc_ref[r0 + B_CHUNK // 2 - 1:r0 + B_CHUNK // 2, :]
            q_c = q[rs]
            k_c = k[rs]
            qe_ref[rs, :] = (q_c * jnp.exp(b - b_mid)).astype(BF16)
            ke_ref[rs, :] = (k_c * jnp.exp(b_mid - b)).astype(BF16)
            kd_ref[rs, :] = (k_c * jnp.exp(b_last - b)).astype(BF16)
            qb_ref[rs, :] = (q_c * jnp.exp(b)).astype(BF16)
            decays.append(jnp.exp(b_last))
        val["decays"] = decays

    def gla_gate():
        bg_ref[...] = _silu(proj(OFF_BG, D_B))

    def gla_state():
        for hd in range(B_HEADS):
            kc_ = slice(hd * B_DK, (hd + 1) * B_DK)
            vc_ = slice(hd * B_DV, (hd + 1) * B_DV)
            s_t = state_ref[hd]
            for c in range(n_chunks):
                rs = slice(c * B_CHUNK, (c + 1) * B_CHUNK)
                sprev_ref[c, hd] = s_t.astype(BF16)
                kv_t = lax.dot_general(v_ref[rs, vc_], kd_ref[rs, kc_], tn,
                                       preferred_element_type=F32)
                s_t = s_t * val["decays"][c][:, kc_] + kv_t
            state_ref[hd] = s_t

    def gla_scores():
        scores = {}
        for blk in blocks:
            for hd in range(B_HEADS):
                kc_ = slice(hd * B_DK, (hd + 1) * B_DK)
                s_hd = lax.dot_general(qe_ref[blk, kc_], ke_ref[blk, kc_], nt,
                                       preferred_element_type=F32)
                scores[blk.start, hd] = jnp.where(same_chunk_causal, s_hd, 0.0).astype(BF16)
        val["scores"] = scores

    def gla_output():
        for blk in blocks:
            for hd in range(B_HEADS):
                kc_ = slice(hd * B_DK, (hd + 1) * B_DK)
                vc_ = slice(hd * B_DV, (hd + 1) * B_DV)
                o_intra = jnp.dot(val["scores"][blk.start, hd], v_ref[blk, vc_],
                                  preferred_element_type=F32)
                o_inter = jnp.concatenate([
                    lax.dot_general(qb_ref[r0:r0 + B_CHUNK, kc_], sprev_ref[r0 // B_CHUNK, hd],
                                    nt, preferred_element_type=F32)
                    for r0 in range(blk.start, blk.stop, B_CHUNK)], axis=0)
                o = o_intra + o_inter
                o_n = o * _rms_scale(o) * bn_g_ref[:, vc_]
                mixed_ref[blk, D_A + hd * B_DV:D_A + (hd + 1) * B_DV] = (
                    o_n * bg_ref[blk, vc_]).astype(BF16)

    def sgu_value():
        v_a = _gelu(proj(OFF_V, D_A))
        mu = jnp.mean(v_a, axis=-1, keepdims=True)
        v_c = v_a - mu
        v_n = v_c * lax.rsqrt(jnp.mean(v_c * v_c, axis=-1, keepdims=True) + EPS)
        vn_ref[...] = (v_n * ln_g_ref[...] + ln_b_ref[...]).astype(BF16)

    def sgu_u():
        val["u_a"] = _gelu(proj(OFF_U, D_A))

    def sgu_gate():
        ug_ref[...] = val["u_a"] * _silu(proj(OFF_AG, D_A))

    def sgu_mix():
        row = lax.broadcasted_iota(jnp.int32, (A_CHUNK, A_CHUNK), 0)
        col = lax.broadcasted_iota(jnp.int32, (A_CHUNK, A_CHUNK), 1)
        causal_a = row >= col
        for g in range(A_GROUPS):
            w_g = jnp.where(causal_a, ws_ref[g], 0.0).astype(BF16)
            gc = slice(g * A_GROUP_DIM, (g + 1) * A_GROUP_DIM)
            for n in range(tile // A_CHUNK):
                rs = slice(n * A_CHUNK, (n + 1) * A_CHUNK)
                sp = jnp.dot(w_g, vn_ref[rs, gc], preferred_element_type=F32) + bs_ref[g]
                mixed_ref[rs, gc] = (ug_ref[rs, gc] * sp).astype(BF16)

    def out_a():
        val["y_a"] = jnp.dot(mixed_ref[:, :D_A], w_out_ref[:D_A, :],
                             preferred_element_type=F32)

    def out_b():
        val["y_b"] = jnp.dot(mixed_ref[:, D_A:], w_out_ref[D_A:, :],
                             preferred_element_type=F32)

    phases = dict(
        gate_lowrank=gate_lowrank, gate_log_decay=gate_log_decay, gla_q=gla_q, gla_k=gla_k,
        gla_cumsum=gla_cumsum, gla_v=gla_v, gla_operands=gla_operands, gla_gate=gla_gate,
        gla_state=gla_state, gla_scores=gla_scores, gla_output=gla_output,
        sgu_value=sgu_value, sgu_u=sgu_u, sgu_gate=sgu_gate, sgu_mix=sgu_mix,
        out_a=out_a, out_b=out_b)
    assert sorted(PHASE_ORDER) == sorted(phases)
    for name in PHASE_ORDER:
        phases[name]()

    y = val["y_a"] + val["y_b"]
    o_ref[...] = x_ref[...] + y * _rms_scale(y) * post_g_ref[...]


def _resident(shape):
    zeros = (0,) * len(shape)
    return pl.BlockSpec(shape, lambda b, s: zeros, pipeline_mode=pl.Buffered(1))


def _hybrid_layer(x, pre_g, w_in, w_a2, b_a2, ln_g, ln_b, w_s, b_s, bn_g, w_out, post_g):
    bsz, seq, d = x.shape
    assert d == D_MODEL and seq % SEQ_TILE == 0
    assert SEQ_TILE % A_CHUNK == 0 and SEQ_TILE % GLA_BLOCK == 0 and GLA_BLOCK % B_CHUNK == 0
    assert w_in.shape == (D_MODEL, D_IN) and w_out.shape == (D_MIX, D_MODEL)
    assert OFF_LR % WEIGHT_STAGE_ROWS == 0 and D_MIX % WEIGHT_STAGE_ROWS == 0

    w_in_t = jnp.swapaxes(w_in, 0, 1).astype(F32)
    w_lr_t_p = jnp.pad(w_in_t[OFF_LR:], ((0, LR_PAD - B_GATE_RANK), (0, 0))).astype(BF16)
    w_a2_p = jnp.pad(w_a2, ((0, LR_PAD - B_GATE_RANK), (0, 0))).astype(BF16)
    row_vec = lambda a: a.reshape(1, -1).astype(F32)

    x_spec = pl.BlockSpec((None, SEQ_TILE, D_MODEL), lambda b, s: (b, s, 0))
    hbm_spec = pl.BlockSpec(memory_space=pl.ANY)
    n_chunks = SEQ_TILE // B_CHUNK
    scratch = [
        pltpu.VMEM((D_MODEL, OFF_LR), BF16),
        pltpu.VMEM((D_MIX, D_MODEL), BF16),
        pltpu.VMEM((WEIGHT_STAGE_SLOTS, WEIGHT_STAGE_ROWS, D_MODEL), F32),
        pltpu.SemaphoreType.DMA((WEIGHT_STAGE_SLOTS,)),
        pltpu.VMEM((SEQ_TILE, D_MODEL), BF16),
        pltpu.VMEM((SEQ_TILE, D_A), BF16),
        pltpu.VMEM((SEQ_TILE, D_A), F32),
        pltpu.VMEM((SEQ_TILE, D_QK), BF16),
        pltpu.VMEM((SEQ_TILE, D_QK), BF16),
        pltpu.VMEM((SEQ_TILE, D_QK), BF16),
        pltpu.VMEM((SEQ_TILE, D_QK), BF16),
        pltpu.VMEM((SEQ_TILE, D_B), BF16),
        pltpu.VMEM((SEQ_TILE, D_B), F32),
        pltpu.VMEM((SEQ_TILE, D_QK), F32),
        pltpu.VMEM((n_chunks, B_HEADS, B_DV, B_DK), BF16),
        pltpu.VMEM((SEQ_TILE, D_MIX), BF16),
        pltpu.VMEM((B_HEADS, B_DV, B_DK), F32),
    ]
    return pl.pallas_call(
        _layer_kernel,
        out_shape=jax.ShapeDtypeStruct(x.shape, x.dtype),
        grid=(bsz, seq // SEQ_TILE),
        in_specs=[
            x_spec,
            _resident((1, D_MODEL)),
            hbm_spec,
            _resident((LR_PAD, D_MODEL)),
            _resident((LR_PAD, D_QK)),
            _resident((1, D_QK)),
            _resident((1, D_A)),
            _resident((1, D_A)),
            _resident((A_GROUPS, A_CHUNK, A_CHUNK)),
            _resident((A_GROUPS, A_CHUNK, 1)),
            _resident((1, D_B)),
            hbm_spec,
            _resident((1, D_MODEL)),
        ],
        out_specs=x_spec,
        scratch_shapes=scratch,
        compiler_params=pltpu.CompilerParams(
            dimension_semantics=("arbitrary", "arbitrary"),
            vmem_limit_bytes=V7X_VMEM_LIMIT_BYTES),
        name="hybrid_gmlp_gla_layer",
    )(x, row_vec(pre_g), w_in_t, w_lr_t_p, w_a2_p, row_vec(b_a2), row_vec(ln_g),
      row_vec(ln_b), w_s.astype(F32), b_s.astype(F32).reshape(A_GROUPS, A_CHUNK, 1),
      row_vec(bn_g), w_out.astype(F32), row_vec(post_g))


def kernel(x, pre_norm_g, w_in, w_a2, b_a2, a_ln_g, a_ln_b, a_w_s, a_b_s, b_norm_g, w_out, post_norm_g):
    for l in range(pre_norm_g.shape[0]):
        x = _hybrid_layer(x, pre_norm_g[l], w_in[l], w_a2[l], b_a2[l], a_ln_g[l], a_ln_b[l],
                          a_w_s[l], a_b_s[l], b_norm_g[l], w_out[l], post_norm_g[l])
    return x
```

```python
import jax
import jax.numpy as jnp
from jax import lax
from jax.experimental import pallas as pl
from jax.experimental.pallas import tpu as pltpu

D_MODEL = 1024
D_MIX = 2 * D_MODEL
D_A = D_MIX // 2
D_B = D_MIX - D_A
A_GROUPS = 8
A_GROUP_DIM = D_A // A_GROUPS
A_CHUNK = 128
B_HEADS = 4
B_DK = D_B // 2 // B_HEADS
B_DV = D_B // B_HEADS
B_GATE_RANK = 16
B_GATE_TAU = 16.0
B_CHUNK = 64
D_QK = B_HEADS * B_DK
EPS = 1e-6

OFF_U = 0
OFF_V = OFF_U + D_A
OFF_AG = OFF_V + D_A
OFF_Q = OFF_AG + D_A
OFF_K = OFF_Q + D_QK
OFF_BV = OFF_K + D_QK
OFF_BG = OFF_BV + D_B
OFF_LR = OFF_BG + D_B
D_IN = OFF_LR + B_GATE_RANK

V7X_LANES = 128
LR_PAD = V7X_LANES

SEQ_TILE = 512
GLA_BLOCK = 128
COL_BLOCK = 512
V7X_VMEM_LIMIT_BYTES = 56 * 1024 * 1024
WEIGHT_STAGE_ROWS = 256
WEIGHT_STAGE_SLOTS = 4

PHASE_ORDER = (
    "gate_lowrank", "sgu_value", "gate_log_decay", "sgu_u", "sgu_gate", "sgu_mix",
    "gla_q", "gla_k", "gla_cumsum", "out_a", "gla_v", "gla_operands", "gla_gate",
    "gla_state", "gla_scores", "gla_output", "out_b",
)

F32 = jnp.float32
BF16 = jnp.bfloat16


def _gelu(x):
    return 0.5 * x * (1.0 + lax.erf(x * (0.5 ** 0.5)))


def _silu(x):
    half = 0.5 * x
    return half + half * jnp.tanh(half)


def _log_sigmoid(x):
    return jnp.minimum(x, 0.0) - jnp.log(1.0 + jnp.exp(-jnp.abs(x)))


def _rms_scale(x):
    return lax.rsqrt(jnp.mean(x * x, axis=-1, keepdims=True) + EPS)


def _load_weights_as_bf16(w_in_t_hbm, w_out_hbm, w_in_ref, w_out_ref, stage_ref, sem):
    slots, rows = stage_ref.shape[:2]
    jobs = [(w_in_t_hbm, r0, True) for r0 in range(0, w_in_ref.shape[1], rows)]
    jobs += [(w_out_hbm, r0, False) for r0 in range(0, w_out_ref.shape[0], rows)]

    def copy(i):
        src, r0, _ = jobs[i]
        return pltpu.make_async_copy(src.at[pl.ds(r0, rows), :],
                                     stage_ref.at[i % slots], sem.at[i % slots])

    for i in range(min(slots - 1, len(jobs))):
        copy(i).start()
    for i, (_, r0, is_w_in) in enumerate(jobs):
        if i + slots - 1 < len(jobs):
            copy(i + slots - 1).start()
        copy(i).wait()
        if is_w_in:
            w_in_ref[:, r0:r0 + rows] = stage_ref[i % slots].T.astype(BF16)
        else:
            w_out_ref[r0:r0 + rows, :] = stage_ref[i % slots].astype(BF16)


def _layer_kernel(x_ref, pre_g_ref, w_in_hbm, w_lr_t_ref, w_a2_ref, b_a2_ref, ln_g_ref, ln_b_ref,
                  ws_ref, bs_ref, bn_g_ref, w_out_hbm, post_g_ref, o_ref,
                  w_in_ref, w_out_ref, stage_ref, stage_sem,
                  h_ref, vn_ref, ug_ref, qe_ref, ke_ref, kd_ref, qb_ref, v_ref, bg_ref,
                  bc_ref, sprev_ref, mixed_ref, state_ref):
    tile = x_ref.shape[0]
    n_chunks = tile // B_CHUNK
    blocks = [slice(r0, r0 + GLA_BLOCK) for r0 in range(0, tile, GLA_BLOCK)]
    nt = (((1,), (1,)), ((), ()))
    tn = (((0,), (0,)), ((), ()))

    @pl.when((pl.program_id(0) == 0) & (pl.program_id(1) == 0))
    def _first_step():
        _load_weights_as_bf16(w_in_hbm, w_out_hbm, w_in_ref, w_out_ref, stage_ref, stage_sem)

    @pl.when(pl.program_id(1) == 0)
    def _reset_state():
        state_ref[...] = jnp.zeros_like(state_ref)

    x = x_ref[...]
    h_ref[...] = (x * _rms_scale(x) * pre_g_ref[...]).astype(BF16)

    def proj(off, width):
        return jnp.dot(h_ref[...], w_in_ref[:, off:off + width],
                       preferred_element_type=F32)

    val = {}

    def gate_lowrank():
        val["lr"] = lax.dot_general(h_ref[...], w_lr_t_ref[...], nt,
                                    preferred_element_type=F32).astype(BF16)

    def gate_log_decay():
        gate_logit = (jnp.dot(val["lr"], w_a2_ref[...], preferred_element_type=F32)
                      + b_a2_ref[...])
        val["log_a"] = _log_sigmoid(gate_logit) * (1.0 / B_GATE_TAU)

    def gla_q():
        val["q"] = proj(OFF_Q, D_QK) * (B_DK ** -0.5)

    def gla_k():
        val["k"] = proj(OFF_K, D_QK)

    r_t = lax.broadcasted_iota(jnp.int32, (GLA_BLOCK, GLA_BLOCK), 0)
    c_t = lax.broadcasted_iota(jnp.int32, (GLA_BLOCK, GLA_BLOCK), 1)
    shift = B_CHUNK.bit_length() - 1
    same_chunk_causal = (r_t >= c_t) & ((r_t >> shift) == (c_t >> shift))

    def gla_cumsum():
        cum_mat = same_chunk_causal.astype(BF16)
        log_a = val["log_a"]
        la_hi = log_a.astype(BF16)
        la_lo = (log_a - la_hi.astype(F32)).astype(BF16)
        for blk in blocks:
            bc_ref[blk, :] = (jnp.dot(cum_mat, la_hi[blk], preferred_element_type=F32)
                              + jnp.dot(cum_mat, la_lo[blk], preferred_element_type=F32))

    def gla_v():
        v_ref[...] = proj(OFF_BV, D_B).astype(BF16)

    def gla_operands():
        q, k = val["q"], val["k"]
        decays = []
        for c in range(n_chunks):
            r0 = c * B_CHUNK
            rs = slice(r0, r0 + B_CHUNK)
            b = bc_ref[rs, :]
            b_last = bc_ref[r0 + B_CHUNK - 1:r0 + B_CHUNK, :]
            b_mid = bc_ref[r0 + B_CHUNK // 2 - 1:r0 + B_CHUNK // 2, :]
            q_c = q[rs]
            k_c = k[rs]
            qe_ref[rs, :] = (q_c * jnp.exp(b - b_mid)).astype(BF16)
            ke_ref[rs, :] = (k_c * jnp.exp(b_mid - b)).astype(BF16)
            kd_ref[rs, :] = (k_c * jnp.exp(b_last - b)).astype(BF16)
            qb_ref[rs, :] = (q_c * jnp.exp(b)).astype(BF16)
            decays.append(jnp.exp(b_last))
        val["decays"] = decays

    def gla_gate():
        bg_ref[...] = _silu(proj(OFF_BG, D_B))

    def gla_state():
        for hd in range(B_HEADS):
            kc_ = slice(hd * B_DK, (hd + 1) * B_DK)
            vc_ = slice(hd * B_DV, (hd + 1) * B_DV)
            s_t = state_ref[hd]
            for c in range(n_chunks):
                rs = slice(c * B_CHUNK, (c + 1) * B_CHUNK)
                sprev_ref[c, hd] = s_t.astype(BF16)
                kv_t = lax.dot_general(v_ref[rs, vc_], kd_ref[rs, kc_], tn,
                                       preferred_element_type=F32)
                s_t = s_t * val["decays"][c][:, kc_] + kv_t
            state_ref[hd] = s_t

    def gla_scores():
        scores = {}
        for blk in blocks:
            for hd in range(B_HEADS):
                kc_ = slice(hd * B_DK, (hd + 1) * B_DK)
                s_hd = lax.dot_general(qe_ref[blk, kc_], ke_ref[blk, kc_], nt,
                                       preferred_element_type=F32)
                scores[blk.start, hd] = jnp.where(same_chunk_causal, s_hd, 0.0).astype(BF16)
        val["scores"] = scores

    def gla_output():
        for blk in blocks:
            for hd in range(B_HEADS):
                kc_ = slice(hd * B_DK, (hd + 1) * B_DK)
                vc_ = slice(hd * B_DV, (hd + 1) * B_DV)
                o_intra = jnp.dot(val["scores"][blk.start, hd], v_ref[blk, vc_],
                                  preferred_element_type=F32)
                o_inter = jnp.concatenate([
                    lax.dot_general(qb_ref[r0:r0 + B_CHUNK, kc_], sprev_ref[r0 // B_CHUNK, hd],
                                    nt, preferred_element_type=F32)
                    for r0 in range(blk.start, blk.stop, B_CHUNK)], axis=0)
                o = o_intra + o_inter
                o_n = o * _rms_scale(o) * bn_g_ref[:, vc_]
                mixed_ref[blk, D_A + hd * B_DV:D_A + (hd + 1) * B_DV] = (
                    o_n * bg_ref[blk, vc_]).astype(BF16)

    def sgu_value():
        v_a = _gelu(proj(OFF_V, D_A))
        mu = jnp.mean(v_a, axis=-1, keepdims=True)
        v_c = v_a - mu
        v_n = v_c * lax.rsqrt(jnp.mean(v_c * v_c, axis=-1, keepdims=True) + EPS)
        vn_ref[...] = (v_n * ln_g_ref[...] + ln_b_ref[...]).astype(BF16)

    def sgu_u():
        pass

    def sgu_gate():
        for c0 in range(0, D_A, COL_BLOCK):
            u_blk = _gelu(proj(OFF_U + c0, COL_BLOCK))
            g_blk = _silu(proj(OFF_AG + c0, COL_BLOCK))
            ug_ref[:, c0:c0 + COL_BLOCK] = u_blk * g_blk

    def sgu_mix():
        row = lax.broadcasted_iota(jnp.int32, (A_CHUNK, A_CHUNK), 0)
        col = lax.broadcasted_iota(jnp.int32, (A_CHUNK, A_CHUNK), 1)
        causal_a = row >= col
        for g in range(A_GROUPS):
            w_g = jnp.where(causal_a, ws_ref[g], 0.0).astype(BF16)
            gc = slice(g * A_GROUP_DIM, (g + 1) * A_GROUP_DIM)
            for n in range(tile // A_CHUNK):
                rs = slice(n * A_CHUNK, (n + 1) * A_CHUNK)
                sp = jnp.dot(w_g, vn_ref[rs, gc], preferred_element_type=F32) + bs_ref[g]
                mixed_ref[rs, gc] = (ug_ref[rs, gc] * sp).astype(BF16)

    def out_a():
        val["y_a"] = jnp.dot(mixed_ref[:, :D_A], w_out_ref[:D_A, :],
                             preferred_element_type=F32)

    def out_b():
        val["y_b"] = jnp.dot(mixed_ref[:, D_A:], w_out_ref[D_A:, :],
                             preferred_element_type=F32)

    phases = dict(
        gate_lowrank=gate_lowrank, gate_log_decay=gate_log_decay, gla_q=gla_q, gla_k=gla_k,
        gla_cumsum=gla_cumsum, gla_v=gla_v, gla_operands=gla_operands, gla_gate=gla_gate,
        gla_state=gla_state, gla_scores=gla_scores, gla_output=gla_output,
        sgu_value=sgu_value, sgu_u=sgu_u, sgu_gate=sgu_gate, sgu_mix=sgu_mix,
        out_a=out_a, out_b=out_b)
    assert sorted(PHASE_ORDER) == sorted(phases)
    for name in PHASE_ORDER:
        phases[name]()

    y = val["y_a"] + val["y_b"]
    o_ref[...] = x_ref[...] + y * _rms_scale(y) * post_g_ref[...]


def _resident(shape):
    zeros = (0,) * len(shape)
    return pl.BlockSpec(shape, lambda b, s: zeros, pipeline_mode=pl.Buffered(1))


def _hybrid_layer(x, pre_g, w_in, w_a2, b_a2, ln_g, ln_b, w_s, b_s, bn_g, w_out, post_g):
    bsz, seq, d = x.shape
    assert d == D_MODEL and seq % SEQ_TILE == 0
    assert SEQ_TILE % A_CHUNK == 0 and SEQ_TILE % GLA_BLOCK == 0 and GLA_BLOCK % B_CHUNK == 0
    assert w_in.shape == (D_MODEL, D_IN) and w_out.shape == (D_MIX, D_MODEL)
    assert OFF_LR % WEIGHT_STAGE_ROWS == 0 and D_MIX % WEIGHT_STAGE_ROWS == 0

    w_in_t = jnp.swapaxes(w_in, 0, 1).astype(F32)
    w_lr_t_p = jnp.pad(w_in_t[OFF_LR:], ((0, LR_PAD - B_GATE_RANK), (0, 0))).astype(BF16)
    w_a2_p = jnp.pad(w_a2, ((0, LR_PAD - B_GATE_RANK), (0, 0))).astype(BF16)
    row_vec = lambda a: a.reshape(1, -1).astype(F32)

    x_spec = pl.BlockSpec((None, SEQ_TILE, D_MODEL), lambda b, s: (b, s, 0))
    hbm_spec = pl.BlockSpec(memory_space=pl.ANY)
    n_chunks = SEQ_TILE // B_CHUNK
    scratch = [
        pltpu.VMEM((D_MODEL, OFF_LR), BF16),
        pltpu.VMEM((D_MIX, D_MODEL), BF16),
        pltpu.VMEM((WEIGHT_STAGE_SLOTS, WEIGHT_STAGE_ROWS, D_MODEL), F32),
        pltpu.SemaphoreType.DMA((WEIGHT_STAGE_SLOTS,)),
        pltpu.VMEM((SEQ_TILE, D_MODEL), BF16),
        pltpu.VMEM((SEQ_TILE, D_A), BF16),
        pltpu.VMEM((SEQ_TILE, D_A), F32),
        pltpu.VMEM((SEQ_TILE, D_QK), BF16),
        pltpu.VMEM((SEQ_TILE, D_QK), BF16),
        pltpu.VMEM((SEQ_TILE, D_QK), BF16),
        pltpu.VMEM((SEQ_TILE, D_QK), BF16),
        pltpu.VMEM((SEQ_TILE, D_B), BF16),
        pltpu.VMEM((SEQ_TILE, D_B), F32),
        pltpu.VMEM((SEQ_TILE, D_QK), F32),
        pltpu.VMEM((n_chunks, B_HEADS, B_DV, B_DK), BF16),
        pltpu.VMEM((SEQ_TILE, D_MIX), BF16),
        pltpu.VMEM((B_HEADS, B_DV, B_DK), F32),
    ]
    return pl.pallas_call(
        _layer_kernel,
        out_shape=jax.ShapeDtypeStruct(x.shape, x.dtype),
        grid=(bsz, seq // SEQ_TILE),
        in_specs=[
            x_spec,
            _resident((1, D_MODEL)),
            hbm_spec,
            _resident((LR_PAD, D_MODEL)),
            _resident((LR_PAD, D_QK)),
            _resident((1, D_QK)),
            _resident((1, D_A)),
            _resident((1, D_A)),
            _resident((A_GROUPS, A_CHUNK, A_CHUNK)),
            _resident((A_GROUPS, A_CHUNK, 1)),
            _resident((1, D_B)),
            hbm_spec,
            _resident((1, D_MODEL)),
        ],
        out_specs=x_spec,
        scratch_shapes=scratch,
        compiler_params=pltpu.CompilerParams(
            dimension_semantics=("arbitrary", "arbitrary"),
            vmem_limit_bytes=V7X_VMEM_LIMIT_BYTES),
        name="hybrid_gmlp_gla_layer",
    )(x, row_vec(pre_g), w_in_t, w_lr_t_p, w_a2_p, row_vec(b_a2), row_vec(ln_g),
      row_vec(ln_b), w_s.astype(F32), b_s.astype(F32).reshape(A_GROUPS, A_CHUNK, 1),
      row_vec(bn_g), w_out.astype(F32), row_vec(post_g))


def kernel(x, pre_norm_g, w_in, w_a2, b_a2, a_ln_g, a_ln_b, a_w_s, a_b_s, b_norm_g, w_out, post_norm_g):
    for l in range(pre_norm_g.shape[0]):
        x = _hybrid_layer(x, pre_norm_g[l], w_in[l], w_a2[l], b_a2[l], a_ln_g[l], a_ln_b[l],
                          a_w_s[l], a_b_s[l], b_norm_g[l], w_out[l], post_norm_g[l])
    return x
```

```python
import jax
import jax.numpy as jnp
from jax import lax
from jax.experimental import pallas as pl
from jax.experimental.pallas import tpu as pltpu

D_MODEL = 1024
D_MIX = 2 * D_MODEL
D_A = D_MIX // 2
D_B = D_MIX - D_A
A_GROUPS = 8
A_GROUP_DIM = D_A // A_GROUPS
A_CHUNK = 128
B_HEADS = 4
B_DK = D_B // 2 // B_HEADS
B_DV = D_B // B_HEADS
B_GATE_RANK = 16
B_GATE_TAU = 16.0
B_CHUNK = 64
D_QK = B_HEADS * B_DK
EPS = 1e-6

OFF_U = 0
OFF_V = OFF_U + D_A
OFF_AG = OFF_V + D_A
OFF_Q = OFF_AG + D_A
OFF_K = OFF_Q + D_QK
OFF_BV = OFF_K + D_QK
OFF_BG = OFF_BV + D_B
OFF_LR = OFF_BG + D_B
D_IN = OFF_LR + B_GATE_RANK

V7X_LANES = 128
LR_PAD = V7X_LANES

SEQ_TILE = 512
GLA_BLOCK = 2 * B_CHUNK
COL_BLOCK = 512
V7X_VMEM_LIMIT_BYTES = 56 * 1024 * 1024
WEIGHT_STAGE_ROWS = 256
WEIGHT_STAGE_SLOTS = 4

PHASE_ORDER = (
    "gate_lowrank", "sgu_value", "gate_log_decay", "sgu_u", "sgu_gate", "sgu_mix",
    "gla_q", "gla_k", "gla_cumsum", "out_a", "gla_v", "gla_operands", "gla_gate",
    "gla_state", "gla_scores", "gla_output", "out_b",
)

F32 = jnp.float32
BF16 = jnp.bfloat16


def _gelu(x):
    return 0.5 * x * (1.0 + lax.erf(x * (0.5 ** 0.5)))


def _silu(x):
    half = 0.5 * x
    return half + half * jnp.tanh(half)


def _log_sigmoid(x):
    return jnp.minimum(x, 0.0) - jnp.log(1.0 + jnp.exp(-jnp.abs(x)))


def _rms_scale(x):
    return lax.rsqrt(jnp.mean(x * x, axis=-1, keepdims=True) + EPS)


def _load_weights_as_bf16(w_in_t_hbm, w_out_hbm, w_in_ref, w_out_ref, stage_ref, sem):
    slots, rows = stage_ref.shape[:2]
    jobs = [(w_in_t_hbm, r0, True) for r0 in range(0, w_in_ref.shape[1], rows)]
    jobs += [(w_out_hbm, r0, False) for r0 in range(0, w_out_ref.shape[0], rows)]

    def copy(i):
        src, r0, _ = jobs[i]
        return pltpu.make_async_copy(src.at[pl.ds(r0, rows), :],
                                     stage_ref.at[i % slots], sem.at[i % slots])

    for i in range(min(slots - 1, len(jobs))):
        copy(i).start()
    for i, (_, r0, is_w_in) in enumerate(jobs):
        if i + slots - 1 < len(jobs):
            copy(i + slots - 1).start()
        copy(i).wait()
        if is_w_in:
            w_in_ref[:, r0:r0 + rows] = stage_ref[i % slots].T.astype(BF16)
        else:
            w_out_ref[r0:r0 + rows, :] = stage_ref[i % slots].astype(BF16)


def _layer_kernel(x_ref, pre_g_ref, w_in_hbm, w_lr_t_ref, w_a2_ref, b_a2_ref, ln_g_ref, ln_b_ref,
                  ws_ref, bs_ref, bn_g_ref, w_out_hbm, post_g_ref, o_ref,
                  w_in_ref, w_out_ref, stage_ref, stage_sem,
                  h_ref, vn_ref, ug_ref, qe_ref, ke_ref, kd_ref, qb_ref, q2_ref, k2_ref, v_ref,
                  bg_ref, bc_ref, sprev_ref, mixed_ref, state_ref):
    tile = x_ref.shape[0]
    blocks = [slice(r0, r0 + GLA_BLOCK) for r0 in range(0, tile, GLA_BLOCK)]
    nt = (((1,), (1,)), ((), ()))
    tn = (((0,), (0,)), ((), ()))

    @pl.when((pl.program_id(0) == 0) & (pl.program_id(1) == 0))
    def _first_step():
        _load_weights_as_bf16(w_in_hbm, w_out_hbm, w_in_ref, w_out_ref, stage_ref, stage_sem)

    @pl.when(pl.program_id(1) == 0)
    def _reset_state():
        state_ref[...] = jnp.zeros_like(state_ref)

    x = x_ref[...]
    h_ref[...] = (x * _rms_scale(x) * pre_g_ref[...]).astype(BF16)

    def proj(off, width):
        return jnp.dot(h_ref[...], w_in_ref[:, off:off + width],
                       preferred_element_type=F32)

    val = {}

    def gate_lowrank():
        val["lr"] = lax.dot_general(h_ref[...], w_lr_t_ref[...], nt,
                                    preferred_element_type=F32).astype(BF16)

    def gate_log_decay():
        gate_logit = (jnp.dot(val["lr"], w_a2_ref[...], preferred_element_type=F32)
                      + b_a2_ref[...])
        val["log_a"] = _log_sigmoid(gate_logit) * (1.0 / B_GATE_TAU)

    def gla_q():
        val["q"] = proj(OFF_Q, D_QK) * (B_DK ** -0.5)

    def gla_k():
        val["k"] = proj(OFF_K, D_QK)

    r_t = lax.broadcasted_iota(jnp.int32, (GLA_BLOCK, GLA_BLOCK), 0)
    c_t = lax.broadcasted_iota(jnp.int32, (GLA_BLOCK, GLA_BLOCK), 1)
    shift = B_CHUNK.bit_length() - 1
    same_chunk_causal = (r_t >= c_t) & ((r_t >> shift) == (c_t >> shift))
    later_chunk = (r_t >> shift) > (c_t >> shift)

    def gla_cumsum():
        cum_mat = same_chunk_causal.astype(BF16)
        log_a = val["log_a"]
        la_hi = log_a.astype(BF16)
        la_lo = (log_a - la_hi.astype(F32)).astype(BF16)
        for blk in blocks:
            bc_ref[blk, :] = (jnp.dot(cum_mat, la_hi[blk], preferred_element_type=F32)
                              + jnp.dot(cum_mat, la_lo[blk], preferred_element_type=F32))

    def gla_v():
        v_ref[...] = proj(OFF_BV, D_B).astype(BF16)

    def gla_operands():
        q, k = val["q"], val["k"]
        pair_decays = []
        for blk in blocks:
            first = slice(blk.start, blk.start + B_CHUNK)
            second = slice(blk.start + B_CHUNK, blk.stop)
            dec = {}
            for rs in (first, second):
                b = bc_ref[rs, :]
                b_last = bc_ref[rs.stop - 1:rs.stop, :]
                b_mid = bc_ref[rs.start + B_CHUNK // 2 - 1:rs.start + B_CHUNK // 2, :]
                q_c = q[rs]
                k_c = k[rs]
                qe_ref[rs, :] = (q_c * jnp.exp(b - b_mid)).astype(BF16)
                ke_ref[rs, :] = (k_c * jnp.exp(b_mid - b)).astype(BF16)
                dec[rs.start] = (jnp.exp(b_last), q_c * jnp.exp(b), k_c * jnp.exp(b_last - b))
            (dec_1, qb_1, kd_1), (dec_2, qb_2, kd_2) = dec[first.start], dec[second.start]
            qb_ref[first, :] = qb_1.astype(BF16)
            qb_ref[second, :] = qb_2.astype(BF16)
            kd_ref[first, :] = kd_1.astype(BF16)
            kd_ref[second, :] = kd_2.astype(BF16)
            q2_ref[first, :] = qb_1.astype(BF16)
            q2_ref[second, :] = (qb_2 * dec_1).astype(BF16)
            k2_ref[first, :] = (kd_1 * dec_2).astype(BF16)
            k2_ref[second, :] = kd_2.astype(BF16)
            pair_decays.append(dec_1 * dec_2)
        val["pair_decays"] = pair_decays

    def gla_gate():
        bg_ref[...] = _silu(proj(OFF_BG, D_B))

    def gla_state():
        for hd in range(B_HEADS):
            kc_ = slice(hd * B_DK, (hd + 1) * B_DK)
            vc_ = slice(hd * B_DV, (hd + 1) * B_DV)
            s_t = state_ref[hd]
            for p, blk in enumerate(blocks):
                sprev_ref[p, hd] = s_t.astype(BF16)
                kv_t = lax.dot_general(v_ref[blk, vc_], k2_ref[blk, kc_], tn,
                                       preferred_element_type=F32)
                s_t = s_t * val["pair_decays"][p][:, kc_] + kv_t
            state_ref[hd] = s_t

    def gla_scores():
        scores = {}
        for blk in blocks:
            for hd in range(B_HEADS):
                kc_ = slice(hd * B_DK, (hd + 1) * B_DK)
                s_same = lax.dot_general(qe_ref[blk, kc_], ke_ref[blk, kc_], nt,
                                         preferred_element_type=F32)
                s_cross = lax.dot_general(qb_ref[blk, kc_], kd_ref[blk, kc_], nt,
                                          preferred_element_type=F32)
                scores[blk.start, hd] = jnp.where(
                    same_chunk_causal, s_same, jnp.where(later_chunk, s_cross, 0.0)).astype(BF16)
        val["scores"] = scores

    def gla_output():
        for p, blk in enumerate(blocks):
            for hd in range(B_HEADS):
                kc_ = slice(hd * B_DK, (hd + 1) * B_DK)
                vc_ = slice(hd * B_DV, (hd + 1) * B_DV)
                o = (jnp.dot(val["scores"][blk.start, hd], v_ref[blk, vc_],
                             preferred_element_type=F32)
                     + lax.dot_general(q2_ref[blk, kc_], sprev_ref[p, hd], nt,
                                       preferred_element_type=F32))
                o_n = o * _rms_scale(o) * bn_g_ref[:, vc_]
                mixed_ref[blk, D_A + hd * B_DV:D_A + (hd + 1) * B_DV] = (
                    o_n * bg_ref[blk, vc_]).astype(BF16)

    def sgu_value():
        v_a = _gelu(proj(OFF_V, D_A))
        mu = jnp.mean(v_a, axis=-1, keepdims=True)
        v_c = v_a - mu
        v_n = v_c * lax.rsqrt(jnp.mean(v_c * v_c, axis=-1, keepdims=True) + EPS)
        vn_ref[...] = (v_n * ln_g_ref[...] + ln_b_ref[...]).astype(BF16)

    def sgu_u():
        pass

    def sgu_gate():
        for c0 in range(0, D_A, COL_BLOCK):
            u_blk = _gelu(proj(OFF_U + c0, COL_BLOCK))
            g_blk = _silu(proj(OFF_AG + c0, COL_BLOCK))
            ug_ref[:, c0:c0 + COL_BLOCK] = u_blk * g_blk

    def sgu_mix():
        row = lax.broadcasted_iota(jnp.int32, (A_CHUNK, A_CHUNK), 0)
        col = lax.broadcasted_iota(jnp.int32, (A_CHUNK, A_CHUNK), 1)
        causal_a = row >= col
        for g in range(A_GROUPS):
            w_g = jnp.where(causal_a, ws_ref[g], 0.0).astype(BF16)
            gc = slice(g * A_GROUP_DIM, (g + 1) * A_GROUP_DIM)
            for n in range(tile // A_CHUNK):
                rs = slice(n * A_CHUNK, (n + 1) * A_CHUNK)
                sp = jnp.dot(w_g, vn_ref[rs, gc], preferred_element_type=F32) + bs_ref[g]
                mixed_ref[rs, gc] = (ug_ref[rs, gc] * sp).astype(BF16)

    def out_a():
        val["y_a"] = jnp.dot(mixed_ref[:, :D_A], w_out_ref[:D_A, :],
                             preferred_element_type=F32)

    def out_b():
        val["y_b"] = jnp.dot(mixed_ref[:, D_A:], w_out_ref[D_A:, :],
                             preferred_element_type=F32)

    phases = dict(
        gate_lowrank=gate_lowrank, gate_log_decay=gate_log_decay, gla_q=gla_q, gla_k=gla_k,
        gla_cumsum=gla_cumsum, gla_v=gla_v, gla_operands=gla_operands, gla_gate=gla_gate,
        gla_state=gla_state, gla_scores=gla_scores, gla_output=gla_output,
        sgu_value=sgu_value, sgu_u=sgu_u, sgu_gate=sgu_gate, sgu_mix=sgu_mix,
        out_a=out_a, out_b=out_b)
    assert sorted(PHASE_ORDER) == sorted(phases)
    for name in PHASE_ORDER:
        phases[name]()

    y = val["y_a"] + val["y_b"]
    o_ref[...] = x_ref[...] + y * _rms_scale(y) * post_g_ref[...]


def _resident(shape):
    zeros = (0,) * len(shape)
    return pl.BlockSpec(shape, lambda b, s: zeros, pipeline_mode=pl.Buffered(1))


def _hybrid_layer(x, pre_g, w_in, w_a2, b_a2, ln_g, ln_b, w_s, b_s, bn_g, w_out, post_g):
    bsz, seq, d = x.shape
    assert d == D_MODEL and seq % SEQ_TILE == 0
    assert SEQ_TILE % A_CHUNK == 0 and SEQ_TILE % GLA_BLOCK == 0 and GLA_BLOCK == 2 * B_CHUNK
    assert w_in.shape == (D_MODEL, D_IN) and w_out.shape == (D_MIX, D_MODEL)
    assert OFF_LR % WEIGHT_STAGE_ROWS == 0 and D_MIX % WEIGHT_STAGE_ROWS == 0

    w_in_t = jnp.swapaxes(w_in, 0, 1).astype(F32)
    w_lr_t_p = jnp.pad(w_in_t[OFF_LR:], ((0, LR_PAD - B_GATE_RANK), (0, 0))).astype(BF16)
    w_a2_p = jnp.pad(w_a2, ((0, LR_PAD - B_GATE_RANK), (0, 0))).astype(BF16)
    row_vec = lambda a: a.reshape(1, -1).astype(F32)

    x_spec = pl.BlockSpec((None, SEQ_TILE, D_MODEL), lambda b, s: (b, s, 0))
    hbm_spec = pl.BlockSpec(memory_space=pl.ANY)
    n_pairs = SEQ_TILE // GLA_BLOCK
    scratch = [
        pltpu.VMEM((D_MODEL, OFF_LR), BF16),
        pltpu.VMEM((D_MIX, D_MODEL), BF16),
        pltpu.VMEM((WEIGHT_STAGE_SLOTS, WEIGHT_STAGE_ROWS, D_MODEL), F32),
        pltpu.SemaphoreType.DMA((WEIGHT_STAGE_SLOTS,)),
        pltpu.VMEM((SEQ_TILE, D_MODEL), BF16),
        pltpu.VMEM((SEQ_TILE, D_A), BF16),
        pltpu.VMEM((SEQ_TILE, D_A), F32),
        pltpu.VMEM((SEQ_TILE, D_QK), BF16),
        pltpu.VMEM((SEQ_TILE, D_QK), BF16),
        pltpu.VMEM((SEQ_TILE, D_QK), BF16),
        pltpu.VMEM((SEQ_TILE, D_QK), BF16),
        pltpu.VMEM((SEQ_TILE, D_QK), BF16),
        pltpu.VMEM((SEQ_TILE, D_QK), BF16),
        pltpu.VMEM((SEQ_TILE, D_B), BF16),
        pltpu.VMEM((SEQ_TILE, D_B), F32),
        pltpu.VMEM((SEQ_TILE, D_QK), F32),
        pltpu.VMEM((n_pairs, B_HEADS, B_DV, B_DK), BF16),
        pltpu.VMEM((SEQ_TILE, D_MIX), BF16),
        pltpu.VMEM((B_HEADS, B_DV, B_DK), F32),
    ]
    return pl.pallas_call(
        _layer_kernel,
        out_shape=jax.ShapeDtypeStruct(x.shape, x.dtype),
        grid=(bsz, seq // SEQ_TILE),
        in_specs=[
            x_spec,
            _resident((1, D_MODEL)),
            hbm_spec,
            _resident((LR_PAD, D_MODEL)),
            _resident((LR_PAD, D_QK)),
            _resident((1, D_QK)),
            _resident((1, D_A)),
            _resident((1, D_A)),
            _resident((A_GROUPS, A_CHUNK, A_CHUNK)),
            _resident((A_GROUPS, A_CHUNK, 1)),
            _resident((1, D_B)),
            hbm_spec,
            _resident((1, D_MODEL)),
        ],
        out_specs=x_spec,
        scratch_shapes=scratch,
        compiler_params=pltpu.CompilerParams(
            dimension_semantics=("arbitrary", "arbitrary"),
            vmem_limit_bytes=V7X_VMEM_LIMIT_BYTES),
        name="hybrid_gmlp_gla_layer",
    )(x, row_vec(pre_g), w_in_t, w_lr_t_p, w_a2_p, row_vec(b_a2), row_vec(ln_g),
      row_vec(ln_b), w_s.astype(F32), b_s.astype(F32).reshape(A_GROUPS, A_CHUNK, 1),
      row_vec(bn_g), w_out.astype(F32), row_vec(post_g))


def kernel(x, pre_norm_g, w_in, w_a2, b_a2, a_ln_g, a_ln_b, a_w_s, a_b_s, b_norm_g, w_out, post_norm_g):
    for l in range(pre_norm_g.shape[0]):
        x = _hybrid_layer(x, pre_norm_g[l], w_in[l], w_a2[l], b_a2[l], a_ln_g[l], a_ln_b[l],
                          a_w_s[l], a_b_s[l], b_norm_g[l], w_out[l], post_norm_g[l])
    return x
```

```python
import jax
import jax.numpy as jnp
from jax import lax
from jax.experimental import pallas as pl
from jax.experimental.pallas import tpu as pltpu

D_MODEL = 1024
D_MIX = 2 * D_MODEL
D_A = D_MIX // 2
D_B = D_MIX - D_A
A_GROUPS = 8
A_GROUP_DIM = D_A // A_GROUPS
A_CHUNK = 128
B_HEADS = 4
B_DK = D_B // 2 // B_HEADS
B_DV = D_B // B_HEADS
B_GATE_RANK = 16
B_GATE_TAU = 16.0
B_CHUNK = 64
D_QK = B_HEADS * B_DK
EPS = 1e-6

OFF_U = 0
OFF_V = OFF_U + D_A
OFF_AG = OFF_V + D_A
OFF_Q = OFF_AG + D_A
OFF_K = OFF_Q + D_QK
OFF_BV = OFF_K + D_QK
OFF_BG = OFF_BV + D_B
OFF_LR = OFF_BG + D_B
D_IN = OFF_LR + B_GATE_RANK

V7X_LANES = 128
LR_PAD = V7X_LANES

SEQ_TILE = 512
GLA_BLOCK = 2 * B_CHUNK
COL_BLOCK = 512
V7X_VMEM_LIMIT_BYTES = 56 * 1024 * 1024
WEIGHT_STAGE_ROWS = 256
WEIGHT_STAGE_SLOTS = 4

PHASE_ORDER = (
    "gate_lowrank", "sgu_value", "gate_log_decay", "sgu_u", "sgu_gate", "sgu_mix",
    "gla_q", "gla_k", "gla_cumsum", "out_a", "gla_v", "gla_operands", "gla_gate",
    "gla_state", "gla_scores", "gla_output", "out_b",
)

F32 = jnp.float32
BF16 = jnp.bfloat16


def _gelu(x):
    return 0.5 * x * (1.0 + lax.erf(x * (0.5 ** 0.5)))


def _silu(x):
    half = 0.5 * x
    return half + half * jnp.tanh(half)


def _log_sigmoid(x):
    return jnp.minimum(x, 0.0) - jnp.log(1.0 + jnp.exp(-jnp.abs(x)))


def _rms_scale(x):
    return lax.rsqrt(jnp.mean(x * x, axis=-1, keepdims=True) + EPS)


def _load_weights_as_bf16(w_in_t_hbm, w_out_hbm, w_in_ref, w_out_ref, stage_ref, sem):
    slots, rows = stage_ref.shape[:2]
    jobs = [(w_in_t_hbm, r0, True) for r0 in range(0, w_in_ref.shape[1], rows)]
    jobs += [(w_out_hbm, r0, False) for r0 in range(0, w_out_ref.shape[0], rows)]

    def copy(i):
        src, r0, _ = jobs[i]
        return pltpu.make_async_copy(src.at[pl.ds(r0, rows), :],
                                     stage_ref.at[i % slots], sem.at[i % slots])

    for i in range(min(slots - 1, len(jobs))):
        copy(i).start()
    for i, (_, r0, is_w_in) in enumerate(jobs):
        if i + slots - 1 < len(jobs):
            copy(i + slots - 1).start()
        copy(i).wait()
        if is_w_in:
            w_in_ref[:, r0:r0 + rows] = stage_ref[i % slots].T.astype(BF16)
        else:
            w_out_ref[r0:r0 + rows, :] = stage_ref[i % slots].astype(BF16)


def _layer_kernel(x_ref, pre_g_ref, w_in_hbm, w_lr_t_ref, w_a2_ref, b_a2_ref, ln_g_ref, ln_b_ref,
                  ws_ref, bs_ref, bn_g_ref, w_out_hbm, post_g_ref, o_ref,
                  w_in_ref, w_out_ref, stage_ref, stage_sem,
                  h_ref, vn_ref, ug_ref, qe_ref, ke_ref, kd_ref, qb_ref, q2_ref, k2_ref, v_ref,
                  bg_ref, bc_ref, sprev_ref, mixed_ref, state_ref):
    tile = x_ref.shape[0]
    blocks = [slice(r0, r0 + GLA_BLOCK) for r0 in range(0, tile, GLA_BLOCK)]
    nt = (((1,), (1,)), ((), ()))
    tn = (((0,), (0,)), ((), ()))

    @pl.when((pl.program_id(0) == 0) & (pl.program_id(1) == 0))
    def _first_step():
        _load_weights_as_bf16(w_in_hbm, w_out_hbm, w_in_ref, w_out_ref, stage_ref, stage_sem)

    @pl.when(pl.program_id(1) == 0)
    def _reset_state():
        state_ref[...] = jnp.zeros_like(state_ref)

    x = x_ref[...]
    h_ref[...] = (x * _rms_scale(x) * pre_g_ref[...]).astype(BF16)

    def proj(off, width):
        return jnp.dot(h_ref[...], w_in_ref[:, off:off + width],
                       preferred_element_type=F32)

    val = {}

    def gate_lowrank():
        val["lr"] = lax.dot_general(h_ref[...], w_lr_t_ref[...], nt,
                                    preferred_element_type=F32).astype(BF16)

    def gate_log_decay():
        gate_logit = (jnp.dot(val["lr"], w_a2_ref[...], preferred_element_type=F32)
                      + b_a2_ref[...])
        val["log_a"] = _log_sigmoid(gate_logit) * (1.0 / B_GATE_TAU)

    def gla_q():
        val["q"] = proj(OFF_Q, D_QK) * (B_DK ** -0.5)

    def gla_k():
        val["k"] = proj(OFF_K, D_QK)

    r_t = lax.broadcasted_iota(jnp.int32, (GLA_BLOCK, GLA_BLOCK), 0)
    c_t = lax.broadcasted_iota(jnp.int32, (GLA_BLOCK, GLA_BLOCK), 1)
    shift = B_CHUNK.bit_length() - 1
    same_chunk_causal = (r_t >= c_t) & ((r_t >> shift) == (c_t >> shift))
    later_chunk = (r_t >> shift) > (c_t >> shift)

    def gla_cumsum():
        cum_mat = same_chunk_causal.astype(BF16)
        log_a = val["log_a"]
        la_hi = log_a.astype(BF16)
        la_lo = (log_a - la_hi.astype(F32)).astype(BF16)
        for blk in blocks:
            bc_ref[blk, :] = (jnp.dot(cum_mat, la_hi[blk], preferred_element_type=F32)
                              + jnp.dot(cum_mat, la_lo[blk], preferred_element_type=F32))

    def gla_v():
        v_ref[...] = proj(OFF_BV, D_B).astype(BF16)

    def gla_operands():
        q, k = val["q"], val["k"]
        pair_decays = []
        for blk in blocks:
            first = slice(blk.start, blk.start + B_CHUNK)
            second = slice(blk.start + B_CHUNK, blk.stop)
            dec = {}
            for rs in (first, second):
                b = bc_ref[rs, :]
                b_last = bc_ref[rs.stop - 1:rs.stop, :]
                b_mid = bc_ref[rs.start + B_CHUNK // 2 - 1:rs.start + B_CHUNK // 2, :]
                q_c = q[rs]
                k_c = k[rs]
                qe_ref[rs, :] = (q_c * jnp.exp(b - b_mid)).astype(BF16)
                ke_ref[rs, :] = (k_c * jnp.exp(b_mid - b)).astype(BF16)
                dec[rs.start] = (jnp.exp(b_last), q_c * jnp.exp(b), k_c * jnp.exp(b_last - b))
            (dec_1, qb_1, kd_1), (dec_2, qb_2, kd_2) = dec[first.start], dec[second.start]
            qb_ref[first, :] = qb_1.astype(BF16)
            qb_ref[second, :] = qb_2.astype(BF16)
            kd_ref[first, :] = kd_1.astype(BF16)
            kd_ref[second, :] = kd_2.astype(BF16)
            q2_ref[first, :] = qb_1.astype(BF16)
            q2_ref[second, :] = (qb_2 * dec_1).astype(BF16)
            k2_ref[first, :] = (kd_1 * dec_2).astype(BF16)
            k2_ref[second, :] = kd_2.astype(BF16)
            pair_decays.append(dec_1 * dec_2)
        val["pair_decays"] = pair_decays

    def gla_gate():
        bg_ref[...] = _silu(proj(OFF_BG, D_B)).astype(BF16)

    def gla_state():
        for hd in range(B_HEADS):
            kc_ = slice(hd * B_DK, (hd + 1) * B_DK)
            vc_ = slice(hd * B_DV, (hd + 1) * B_DV)
            s_t = state_ref[hd]
            for p, blk in enumerate(blocks):
                sprev_ref[p, hd] = s_t.astype(BF16)
                kv_t = lax.dot_general(v_ref[blk, vc_], k2_ref[blk, kc_], tn,
                                       preferred_element_type=F32)
                s_t = s_t * val["pair_decays"][p][:, kc_] + kv_t
            state_ref[hd] = s_t

    def gla_scores():
        scores = {}
        for blk in blocks:
            for hd in range(B_HEADS):
                kc_ = slice(hd * B_DK, (hd + 1) * B_DK)
                s_same = lax.dot_general(qe_ref[blk, kc_], ke_ref[blk, kc_], nt,
                                         preferred_element_type=F32)
                s_cross = lax.dot_general(qb_ref[blk, kc_], kd_ref[blk, kc_], nt,
                                          preferred_element_type=F32)
                scores[blk.start, hd] = jnp.where(
                    same_chunk_causal, s_same, jnp.where(later_chunk, s_cross, 0.0)).astype(BF16)
        val["scores"] = scores

    def gla_output():
        for p, blk in enumerate(blocks):
            for hd in range(B_HEADS):
                kc_ = slice(hd * B_DK, (hd + 1) * B_DK)
                vc_ = slice(hd * B_DV, (hd + 1) * B_DV)
                o = (jnp.dot(val["scores"][blk.start, hd], v_ref[blk, vc_],
                             preferred_element_type=F32)
                     + lax.dot_general(q2_ref[blk, kc_], sprev_ref[p, hd], nt,
                                       preferred_element_type=F32))
                o_n = o * _rms_scale(o) * bn_g_ref[:, vc_]
                mixed_ref[blk, D_A + hd * B_DV:D_A + (hd + 1) * B_DV] = (
                    o_n * bg_ref[blk, vc_].astype(F32)).astype(BF16)

    def sgu_value():
        v_a = _gelu(proj(OFF_V, D_A))
        mu = jnp.mean(v_a, axis=-1, keepdims=True)
        v_c = v_a - mu
        v_n = v_c * lax.rsqrt(jnp.mean(v_c * v_c, axis=-1, keepdims=True) + EPS)
        vn_ref[...] = (v_n * ln_g_ref[...] + ln_b_ref[...]).astype(BF16)

    def sgu_u():
        pass

    def sgu_gate():
        for c0 in range(0, D_A, COL_BLOCK):
            u_blk = _gelu(proj(OFF_U + c0, COL_BLOCK))
            g_blk = _silu(proj(OFF_AG + c0, COL_BLOCK))
            ug_ref[:, c0:c0 + COL_BLOCK] = (u_blk * g_blk).astype(BF16)

    def sgu_mix():
        row = lax.broadcasted_iota(jnp.int32, (A_CHUNK, A_CHUNK), 0)
        col = lax.broadcasted_iota(jnp.int32, (A_CHUNK, A_CHUNK), 1)
        causal_a = row >= col
        for g in range(A_GROUPS):
            w_g = jnp.where(causal_a, ws_ref[g], 0.0).astype(BF16)
            gc = slice(g * A_GROUP_DIM, (g + 1) * A_GROUP_DIM)
            for n in range(tile // A_CHUNK):
                rs = slice(n * A_CHUNK, (n + 1) * A_CHUNK)
                sp = jnp.dot(w_g, vn_ref[rs, gc], preferred_element_type=F32) + bs_ref[g]
                mixed_ref[rs, gc] = (ug_ref[rs, gc].astype(F32) * sp).astype(BF16)

    def out_a():
        val["y_a"] = jnp.dot(mixed_ref[:, :D_A], w_out_ref[:D_A, :],
                             preferred_element_type=F32)

    def out_b():
        val["y_b"] = jnp.dot(mixed_ref[:, D_A:], w_out_ref[D_A:, :],
                             preferred_element_type=F32)

    phases = dict(
        gate_lowrank=gate_lowrank, gate_log_decay=gate_log_decay, gla_q=gla_q, gla_k=gla_k,
        gla_cumsum=gla_cumsum, gla_v=gla_v, gla_operands=gla_operands, gla_gate=gla_gate,
        gla_state=gla_state, gla_scores=gla_scores, gla_output=gla_output,
        sgu_value=sgu_value, sgu_u=sgu_u, sgu_gate=sgu_gate, sgu_mix=sgu_mix,
        out_a=out_a, out_b=out_b)
    assert sorted(PHASE_ORDER) == sorted(phases)
    for name in PHASE_ORDER:
        phases[name]()

    y = val["y_a"] + val["y_b"]
    o_ref[...] = x_ref[...] + y * _rms_scale(y) * post_g_ref[...]


def _resident(shape):
    zeros = (0,) * len(shape)
    return pl.BlockSpec(shape, lambda b, s: zeros, pipeline_mode=pl.Buffered(1))


def _hybrid_layer(x, pre_g, w_in, w_a2, b_a2, ln_g, ln_b, w_s, b_s, bn_g, w_out, post_g):
    bsz, seq, d = x.shape
    assert d == D_MODEL and seq % SEQ_TILE == 0
    assert SEQ_TILE % A_CHUNK == 0 and SEQ_TILE % GLA_BLOCK == 0 and GLA_BLOCK == 2 * B_CHUNK
    assert w_in.shape == (D_MODEL, D_IN) and w_out.shape == (D_MIX, D_MODEL)
    assert OFF_LR % WEIGHT_STAGE_ROWS == 0 and D_MIX % WEIGHT_STAGE_ROWS == 0

    w_in_t = jnp.swapaxes(w_in, 0, 1).astype(F32)
    w_lr_t_p = jnp.pad(w_in_t[OFF_LR:], ((0, LR_PAD - B_GATE_RANK), (0, 0))).astype(BF16)
    w_a2_p = jnp.pad(w_a2, ((0, LR_PAD - B_GATE_RANK), (0, 0))).astype(BF16)
    row_vec = lambda a: a.reshape(1, -1).astype(F32)

    x_spec = pl.BlockSpec((None, SEQ_TILE, D_MODEL), lambda b, s: (b, s, 0))
    hbm_spec = pl.BlockSpec(memory_space=pl.ANY)
    n_pairs = SEQ_TILE // GLA_BLOCK
    scratch = [
        pltpu.VMEM((D_MODEL, OFF_LR), BF16),
        pltpu.VMEM((D_MIX, D_MODEL), BF16),
        pltpu.VMEM((WEIGHT_STAGE_SLOTS, WEIGHT_STAGE_ROWS, D_MODEL), F32),
        pltpu.SemaphoreType.DMA((WEIGHT_STAGE_SLOTS,)),
        pltpu.VMEM((SEQ_TILE, D_MODEL), BF16),
        pltpu.VMEM((SEQ_TILE, D_A), BF16),
        pltpu.VMEM((SEQ_TILE, D_A), BF16),
        pltpu.VMEM((SEQ_TILE, D_QK), BF16),
        pltpu.VMEM((SEQ_TILE, D_QK), BF16),
        pltpu.VMEM((SEQ_TILE, D_QK), BF16),
        pltpu.VMEM((SEQ_TILE, D_QK), BF16),
        pltpu.VMEM((SEQ_TILE, D_QK), BF16),
        pltpu.VMEM((SEQ_TILE, D_QK), BF16),
        pltpu.VMEM((SEQ_TILE, D_B), BF16),
        pltpu.VMEM((SEQ_TILE, D_B), BF16),
        pltpu.VMEM((SEQ_TILE, D_QK), F32),
        pltpu.VMEM((n_pairs, B_HEADS, B_DV, B_DK), BF16),
        pltpu.VMEM((SEQ_TILE, D_MIX), BF16),
        pltpu.VMEM((B_HEADS, B_DV, B_DK), F32),
    ]
    return pl.pallas_call(
        _layer_kernel,
        out_shape=jax.ShapeDtypeStruct(x.shape, x.dtype),
        grid=(bsz, seq // SEQ_TILE),
        in_specs=[
            x_spec,
            _resident((1, D_MODEL)),
            hbm_spec,
            _resident((LR_PAD, D_MODEL)),
            _resident((LR_PAD, D_QK)),
            _resident((1, D_QK)),
            _resident((1, D_A)),
            _resident((1, D_A)),
            _resident((A_GROUPS, A_CHUNK, A_CHUNK)),
            _resident((A_GROUPS, A_CHUNK, 1)),
            _resident((1, D_B)),
            hbm_spec,
            _resident((1, D_MODEL)),
        ],
        out_specs=x_spec,
        scratch_shapes=scratch,
        compiler_params=pltpu.CompilerParams(
            dimension_semantics=("arbitrary", "arbitrary"),
            vmem_limit_bytes=V7X_VMEM_LIMIT_BYTES),
        name="hybrid_gmlp_gla_layer",
    )(x, row_vec(pre_g), w_in_t, w_lr_t_p, w_a2_p, row_vec(b_a2), row_vec(ln_g),
      row_vec(ln_b), w_s.astype(F32), b_s.astype(F32).reshape(A_GROUPS, A_CHUNK, 1),
      row_vec(bn_g), w_out.astype(F32), row_vec(post_g))


def kernel(x, pre_norm_g, w_in, w_a2, b_a2, a_ln_g, a_ln_b, a_w_s, a_b_s, b_norm_g, w_out, post_norm_g):
    for l in range(pre_norm_g.shape[0]):
        x = _hybrid_layer(x, pre_norm_g[l], w_in[l], w_a2[l], b_a2[l], a_ln_g[l], a_ln_b[l],
                          a_w_s[l], a_b_s[l], b_norm_g[l], w_out[l], post_norm_g[l])
    return x
```

```python
import jax
import jax.numpy as jnp
from jax import lax
from jax.experimental import pallas as pl
from jax.experimental.pallas import tpu as pltpu

D_MODEL = 1024
D_MIX = 2 * D_MODEL
D_A = D_MIX // 2
D_B = D_MIX - D_A
A_GROUPS = 8
A_GROUP_DIM = D_A // A_GROUPS
A_CHUNK = 128
B_HEADS = 4
B_DK = D_B // 2 // B_HEADS
B_DV = D_B // B_HEADS
B_GATE_RANK = 16
B_GATE_TAU = 16.0
B_CHUNK = 64
D_QK = B_HEADS * B_DK
EPS = 1e-6

OFF_U = 0
OFF_V = OFF_U + D_A
OFF_AG = OFF_V + D_A
OFF_Q = OFF_AG + D_A
OFF_K = OFF_Q + D_QK
OFF_BV = OFF_K + D_QK
OFF_BG = OFF_BV + D_B
OFF_LR = OFF_BG + D_B
D_IN = OFF_LR + B_GATE_RANK

V7X_LANES = 128
LR_PAD = V7X_LANES

SEQ_TILE = 512
GLA_BLOCK = 2 * B_CHUNK
COL_BLOCK = 512
V7X_VMEM_LIMIT_BYTES = 56 * 1024 * 1024
WEIGHT_STAGE_ROWS = 256
WEIGHT_STAGE_SLOTS = 4

PHASE_ORDER = (
    "gate_lowrank", "sgu_value", "gate_log_decay", "sgu_u", "sgu_gate", "sgu_mix",
    "gla_q", "gla_k", "gla_cumsum", "out_a", "gla_v", "gla_operands", "gla_gate",
    "gla_state", "gla_scores", "gla_output", "out_b",
)

F32 = jnp.float32
BF16 = jnp.bfloat16


def _gelu(x):
    return 0.5 * x * (1.0 + lax.erf(x * (0.5 ** 0.5)))


def _silu(x):
    half = 0.5 * x
    return half + half * jnp.tanh(half)


def _log_sigmoid(x):
    return jnp.minimum(x, 0.0) - jnp.log(1.0 + jnp.exp(-jnp.abs(x)))


def _rms_scale(x):
    return lax.rsqrt(jnp.mean(x * x, axis=-1, keepdims=True) + EPS)


def _load_weights_as_bf16(w_in_t_hbm, w_out_hbm, w_in_ref, w_out_ref, stage_ref, sem):
    slots, rows = stage_ref.shape[:2]
    jobs = [(w_in_t_hbm, r0, True) for r0 in range(0, w_in_ref.shape[1], rows)]
    jobs += [(w_out_hbm, r0, False) for r0 in range(0, w_out_ref.shape[0], rows)]

    def copy(i):
        src, r0, _ = jobs[i]
        return pltpu.make_async_copy(src.at[pl.ds(r0, rows), :],
                                     stage_ref.at[i % slots], sem.at[i % slots])

    for i in range(min(slots - 1, len(jobs))):
        copy(i).start()
    for i, (_, r0, is_w_in) in enumerate(jobs):
        if i + slots - 1 < len(jobs):
            copy(i + slots - 1).start()
        copy(i).wait()
        if is_w_in:
            w_in_ref[:, r0:r0 + rows] = stage_ref[i % slots].T.astype(BF16)
        else:
            w_out_ref[r0:r0 + rows, :] = stage_ref[i % slots].astype(BF16)


def _prepare_small_operands(w_in_t_hbm, w_a2_in_ref, bs_in_ref, w_lr_t_ref, w_a2_ref, bs_ref,
                            lr_stage_ref, sem):
    lr_copy = pltpu.make_async_copy(w_in_t_hbm.at[pl.ds(OFF_LR, B_GATE_RANK), :],
                                    lr_stage_ref, sem)
    lr_copy.start()
    w_lr_t_ref[...] = jnp.zeros_like(w_lr_t_ref)
    w_a2_ref[...] = jnp.zeros_like(w_a2_ref)
    w_a2_ref[:B_GATE_RANK, :] = w_a2_in_ref[...].astype(BF16)
    row = lax.broadcasted_iota(jnp.int32, (A_CHUNK, A_CHUNK), 0)
    col = lax.broadcasted_iota(jnp.int32, (A_CHUNK, A_CHUNK), 1)
    ones = jnp.ones((A_CHUNK, A_CHUNK), BF16)
    for g in range(A_GROUPS):
        rest = jnp.where(row == col, bs_in_ref[g:g + 1, :], 0.0)
        acc = jnp.zeros((A_CHUNK, A_CHUNK), F32)
        for _ in range(3):
            part = rest.astype(BF16)
            acc = acc + jnp.dot(part, ones, preferred_element_type=F32)
            rest = rest - part.astype(F32)
        bs_ref[g] = acc
    lr_copy.wait()
    w_lr_t_ref[:B_GATE_RANK, :] = lr_stage_ref[...].astype(BF16)


def _layer_kernel(x_ref, pre_g_ref, w_in_hbm, w_a2_in_ref, b_a2_ref, ln_g_ref, ln_b_ref,
                  ws_ref, bs_in_ref, bn_g_ref, w_out_hbm, post_g_ref, o_ref,
                  w_in_ref, w_out_ref, stage_ref, stage_sem,
                  w_lr_t_ref, w_a2_ref, bs_ref, lr_stage_ref, lr_sem,
                  h_ref, vn_ref, ug_ref, qe_ref, ke_ref, kd_ref, qb_ref, q2_ref, k2_ref, v_ref,
                  bg_ref, bc_ref, sprev_ref, mixed_ref, state_ref):
    tile = x_ref.shape[0]
    blocks = [slice(r0, r0 + GLA_BLOCK) for r0 in range(0, tile, GLA_BLOCK)]
    nt = (((1,), (1,)), ((), ()))
    tn = (((0,), (0,)), ((), ()))

    @pl.when((pl.program_id(0) == 0) & (pl.program_id(1) == 0))
    def _first_step():
        _prepare_small_operands(w_in_hbm, w_a2_in_ref, bs_in_ref, w_lr_t_ref, w_a2_ref, bs_ref,
                                lr_stage_ref, lr_sem)
        _load_weights_as_bf16(w_in_hbm, w_out_hbm, w_in_ref, w_out_ref, stage_ref, stage_sem)

    @pl.when(pl.program_id(1) == 0)
    def _reset_state():
        state_ref[...] = jnp.zeros_like(state_ref)

    x = x_ref[...]
    h_ref[...] = (x * _rms_scale(x) * pre_g_ref[...]).astype(BF16)

    def proj(off, width):
        return jnp.dot(h_ref[...], w_in_ref[:, off:off + width],
                       preferred_element_type=F32)

    val = {}

    def gate_lowrank():
        val["lr"] = lax.dot_general(h_ref[...], w_lr_t_ref[...], nt,
                                    preferred_element_type=F32).astype(BF16)

    def gate_log_decay():
        gate_logit = (jnp.dot(val["lr"], w_a2_ref[...], preferred_element_type=F32)
                      + b_a2_ref[...])
        val["log_a"] = _log_sigmoid(gate_logit) * (1.0 / B_GATE_TAU)

    def gla_q():
        val["q"] = proj(OFF_Q, D_QK) * (B_DK ** -0.5)

    def gla_k():
        val["k"] = proj(OFF_K, D_QK)

    r_t = lax.broadcasted_iota(jnp.int32, (GLA_BLOCK, GLA_BLOCK), 0)
    c_t = lax.broadcasted_iota(jnp.int32, (GLA_BLOCK, GLA_BLOCK), 1)
    shift = B_CHUNK.bit_length() - 1
    same_chunk_causal = (r_t >= c_t) & ((r_t >> shift) == (c_t >> shift))
    later_chunk = (r_t >> shift) > (c_t >> shift)

    def gla_cumsum():
        cum_mat = same_chunk_causal.astype(BF16)
        log_a = val["log_a"]
        la_hi = log_a.astype(BF16)
        la_lo = (log_a - la_hi.astype(F32)).astype(BF16)
        for blk in blocks:
            bc_ref[blk, :] = (jnp.dot(cum_mat, la_hi[blk], preferred_element_type=F32)
                              + jnp.dot(cum_mat, la_lo[blk], preferred_element_type=F32))

    def gla_v():
        v_ref[...] = proj(OFF_BV, D_B).astype(BF16)

    def gla_operands():
        q, k = val["q"], val["k"]
        pair_decays = []
        for blk in blocks:
            first = slice(blk.start, blk.start + B_CHUNK)
            second = slice(blk.start + B_CHUNK, blk.stop)
            dec = {}
            for rs in (first, second):
                b = bc_ref[rs, :]
                b_last = bc_ref[rs.stop - 1:rs.stop, :]
                b_mid = bc_ref[rs.start + B_CHUNK // 2 - 1:rs.start + B_CHUNK // 2, :]
                q_c = q[rs]
                k_c = k[rs]
                qe_ref[rs, :] = (q_c * jnp.exp(b - b_mid)).astype(BF16)
                ke_ref[rs, :] = (k_c * jnp.exp(b_mid - b)).astype(BF16)
                dec[rs.start] = (jnp.exp(b_last), q_c * jnp.exp(b), k_c * jnp.exp(b_last - b))
            (dec_1, qb_1, kd_1), (dec_2, qb_2, kd_2) = dec[first.start], dec[second.start]
            qb_ref[first, :] = qb_1.astype(BF16)
            qb_ref[second, :] = qb_2.astype(BF16)
            kd_ref[first, :] = kd_1.astype(BF16)
            kd_ref[second, :] = kd_2.astype(BF16)
            q2_ref[first, :] = qb_1.astype(BF16)
            q2_ref[second, :] = (qb_2 * dec_1).astype(BF16)
            k2_ref[first, :] = (kd_1 * dec_2).astype(BF16)
            k2_ref[second, :] = kd_2.astype(BF16)
            pair_decays.append(dec_1 * dec_2)
        val["pair_decays"] = pair_decays

    def gla_gate():
        bg_ref[...] = _silu(proj(OFF_BG, D_B))

    def gla_state():
        for hd in range(B_HEADS):
            kc_ = slice(hd * B_DK, (hd + 1) * B_DK)
            vc_ = slice(hd * B_DV, (hd + 1) * B_DV)
            s_t = state_ref[hd]
            for p, blk in enumerate(blocks):
                sprev_ref[p, hd] = s_t.astype(BF16)
                kv_t = lax.dot_general(v_ref[blk, vc_], k2_ref[blk, kc_], tn,
                                       preferred_element_type=F32)
                s_t = s_t * val["pair_decays"][p][:, kc_] + kv_t
            state_ref[hd] = s_t

    def gla_scores():
        scores = {}
        for blk in blocks:
            for hd in range(B_HEADS):
                kc_ = slice(hd * B_DK, (hd + 1) * B_DK)
                s_same = lax.dot_general(qe_ref[blk, kc_], ke_ref[blk, kc_], nt,
                                         preferred_element_type=F32)
                s_cross = lax.dot_general(qb_ref[blk, kc_], kd_ref[blk, kc_], nt,
                                          preferred_element_type=F32)
                scores[blk.start, hd] = jnp.where(
                    same_chunk_causal, s_same, jnp.where(later_chunk, s_cross, 0.0)).astype(BF16)
        val["scores"] = scores

    def gla_output():
        for p, blk in enumerate(blocks):
            for hd in range(B_HEADS):
                kc_ = slice(hd * B_DK, (hd + 1) * B_DK)
                vc_ = slice(hd * B_DV, (hd + 1) * B_DV)
                o = (jnp.dot(val["scores"][blk.start, hd], v_ref[blk, vc_],
                             preferred_element_type=F32)
                     + lax.dot_general(q2_ref[blk, kc_], sprev_ref[p, hd], nt,
                                       preferred_element_type=F32))
                o_n = o * _rms_scale(o) * bn_g_ref[:, vc_]
                mixed_ref[blk, D_A + hd * B_DV:D_A + (hd + 1) * B_DV] = (
                    o_n * bg_ref[blk, vc_]).astype(BF16)

    def sgu_value():
        v_a = _gelu(proj(OFF_V, D_A))
        mu = jnp.mean(v_a, axis=-1, keepdims=True)
        v_c = v_a - mu
        v_n = v_c * lax.rsqrt(jnp.mean(v_c * v_c, axis=-1, keepdims=True) + EPS)
        vn_ref[...] = (v_n * ln_g_ref[...] + ln_b_ref[...]).astype(BF16)

    def sgu_u():
        pass

    def sgu_gate():
        for c0 in range(0, D_A, COL_BLOCK):
            u_blk = _gelu(proj(OFF_U + c0, COL_BLOCK))
            g_blk = _silu(proj(OFF_AG + c0, COL_BLOCK))
            ug_ref[:, c0:c0 + COL_BLOCK] = u_blk * g_blk

    def sgu_mix():
        row = lax.broadcasted_iota(jnp.int32, (A_CHUNK, A_CHUNK), 0)
        col = lax.broadcasted_iota(jnp.int32, (A_CHUNK, A_CHUNK), 1)
        causal_a = row >= col
        for g in range(A_GROUPS):
            w_g = jnp.where(causal_a, ws_ref[g], 0.0).astype(BF16)
            gc = slice(g * A_GROUP_DIM, (g + 1) * A_GROUP_DIM)
            for n in range(tile // A_CHUNK):
                rs = slice(n * A_CHUNK, (n + 1) * A_CHUNK)
                sp = jnp.dot(w_g, vn_ref[rs, gc], preferred_element_type=F32) + bs_ref[g]
                mixed_ref[rs, gc] = (ug_ref[rs, gc] * sp).astype(BF16)

    def out_a():
        val["y_a"] = jnp.dot(mixed_ref[:, :D_A], w_out_ref[:D_A, :],
                             preferred_element_type=F32)

    def out_b():
        val["y_b"] = jnp.dot(mixed_ref[:, D_A:], w_out_ref[D_A:, :],
                             preferred_element_type=F32)

    phases = dict(
        gate_lowrank=gate_lowrank, gate_log_decay=gate_log_decay, gla_q=gla_q, gla_k=gla_k,
        gla_cumsum=gla_cumsum, gla_v=gla_v, gla_operands=gla_operands, gla_gate=gla_gate,
        gla_state=gla_state, gla_scores=gla_scores, gla_output=gla_output,
        sgu_value=sgu_value, sgu_u=sgu_u, sgu_gate=sgu_gate, sgu_mix=sgu_mix,
        out_a=out_a, out_b=out_b)
    assert sorted(PHASE_ORDER) == sorted(phases)
    for name in PHASE_ORDER:
        phases[name]()

    y = val["y_a"] + val["y_b"]
    o_ref[...] = x_ref[...] + y * _rms_scale(y) * post_g_ref[...]


def _resident(shape):
    zeros = (0,) * len(shape)
    return pl.BlockSpec(shape, lambda b, s: zeros, pipeline_mode=pl.Buffered(1))


def _hybrid_layer(x, pre_g, w_in, w_a2, b_a2, ln_g, ln_b, w_s, b_s, bn_g, w_out, post_g):
    bsz, seq, d = x.shape
    assert d == D_MODEL and seq % SEQ_TILE == 0
    assert SEQ_TILE % A_CHUNK == 0 and SEQ_TILE % GLA_BLOCK == 0 and GLA_BLOCK == 2 * B_CHUNK
    assert w_in.shape == (D_MODEL, D_IN) and w_out.shape == (D_MIX, D_MODEL)
    assert OFF_LR % WEIGHT_STAGE_ROWS == 0 and D_MIX % WEIGHT_STAGE_ROWS == 0

    w_in_t = jnp.swapaxes(w_in, 0, 1).astype(F32)
    row_vec = lambda a: a.reshape(1, -1).astype(F32)

    x_spec = pl.BlockSpec((None, SEQ_TILE, D_MODEL), lambda b, s: (b, s, 0))
    hbm_spec = pl.BlockSpec(memory_space=pl.ANY)
    n_pairs = SEQ_TILE // GLA_BLOCK
    scratch = [
        pltpu.VMEM((D_MODEL, OFF_LR), BF16),
        pltpu.VMEM((D_MIX, D_MODEL), BF16),
        pltpu.VMEM((WEIGHT_STAGE_SLOTS, WEIGHT_STAGE_ROWS, D_MODEL), F32),
        pltpu.SemaphoreType.DMA((WEIGHT_STAGE_SLOTS,)),
        pltpu.VMEM((LR_PAD, D_MODEL), BF16),
        pltpu.VMEM((LR_PAD, D_QK), BF16),
        pltpu.VMEM((A_GROUPS, A_CHUNK, A_CHUNK), F32),
        pltpu.VMEM((B_GATE_RANK, D_MODEL), F32),
        pltpu.SemaphoreType.DMA(()),
        pltpu.VMEM((SEQ_TILE, D_MODEL), BF16),
        pltpu.VMEM((SEQ_TILE, D_A), BF16),
        pltpu.VMEM((SEQ_TILE, D_A), F32),
        pltpu.VMEM((SEQ_TILE, D_QK), BF16),
        pltpu.VMEM((SEQ_TILE, D_QK), BF16),
        pltpu.VMEM((SEQ_TILE, D_QK), BF16),
        pltpu.VMEM((SEQ_TILE, D_QK), BF16),
        pltpu.VMEM((SEQ_TILE, D_QK), BF16),
        pltpu.VMEM((SEQ_TILE, D_QK), BF16),
        pltpu.VMEM((SEQ_TILE, D_B), BF16),
        pltpu.VMEM((SEQ_TILE, D_B), F32),
        pltpu.VMEM((SEQ_TILE, D_QK), F32),
        pltpu.VMEM((n_pairs, B_HEADS, B_DV, B_DK), BF16),
        pltpu.VMEM((SEQ_TILE, D_MIX), BF16),
        pltpu.VMEM((B_HEADS, B_DV, B_DK), F32),
    ]
    return pl.pallas_call(
        _layer_kernel,
        out_shape=jax.ShapeDtypeStruct(x.shape, x.dtype),
        grid=(bsz, seq // SEQ_TILE),
        in_specs=[
            x_spec,
            _resident((1, D_MODEL)),
            hbm_spec,
            _resident((B_GATE_RANK, D_QK)),
            _resident((1, D_QK)),
            _resident((1, D_A)),
            _resident((1, D_A)),
            _resident((A_GROUPS, A_CHUNK, A_CHUNK)),
            _resident((A_GROUPS, A_CHUNK)),
            _resident((1, D_B)),
            hbm_spec,
            _resident((1, D_MODEL)),
        ],
        out_specs=x_spec,
        scratch_shapes=scratch,
        compiler_params=pltpu.CompilerParams(
            dimension_semantics=("arbitrary", "arbitrary"),
            vmem_limit_bytes=V7X_VMEM_LIMIT_BYTES),
        name="hybrid_gmlp_gla_layer",
    )(x, row_vec(pre_g), w_in_t, w_a2.astype(F32), row_vec(b_a2), row_vec(ln_g),
      row_vec(ln_b), w_s.astype(F32), b_s.astype(F32), row_vec(bn_g), w_out.astype(F32),
      row_vec(post_g))


def kernel(x, pre_norm_g, w_in, w_a2, b_a2, a_ln_g, a_ln_b, a_w_s, a_b_s, b_norm_g, w_out, post_norm_g):
    for l in range(pre_norm_g.shape[0]):
        x = _hybrid_layer(x, pre_norm_g[l], w_in[l], w_a2[l], b_a2[l], a_ln_g[l], a_ln_b[l],
                          a_w_s[l], a_b_s[l], b_norm_g[l], w_out[l], post_norm_g[l])
    return x
```

```python
import jax
import jax.numpy as jnp
from jax import lax
from jax.experimental import pallas as pl
from jax.experimental.pallas import tpu as pltpu

D_MODEL = 1024
D_MIX = 2 * D_MODEL
D_A = D_MIX // 2
D_B = D_MIX - D_A
A_GROUPS = 8
A_GROUP_DIM = D_A // A_GROUPS
A_CHUNK = 128
B_HEADS = 4
B_DK = D_B // 2 // B_HEADS
B_DV = D_B // B_HEADS
B_GATE_RANK = 16
B_GATE_TAU = 16.0
B_CHUNK = 64
D_QK = B_HEADS * B_DK
EPS = 1e-6

OFF_U = 0
OFF_V = OFF_U + D_A
OFF_AG = OFF_V + D_A
OFF_Q = OFF_AG + D_A
OFF_K = OFF_Q + D_QK
OFF_BV = OFF_K + D_QK
OFF_BG = OFF_BV + D_B
OFF_LR = OFF_BG + D_B
D_IN = OFF_LR + B_GATE_RANK

V7X_LANES = 128
LR_PAD = V7X_LANES

SEQ_TILE = 512
GLA_BLOCK = 2 * B_CHUNK
COL_BLOCK = 512
V7X_VMEM_LIMIT_BYTES = 56 * 1024 * 1024
WEIGHT_STAGE_ROWS = 256
WEIGHT_STAGE_SLOTS = 4

PHASE_ORDER = (
    "gate_lowrank", "sgu_value", "gate_log_decay", "sgu_u", "sgu_gate", "sgu_mix",
    "gla_q", "gla_k", "gla_cumsum", "out_a", "gla_v", "gla_operands", "gla_gate",
    "gla_state", "gla_scores", "gla_output", "out_b",
)

F32 = jnp.float32
BF16 = jnp.bfloat16


def _gelu(x):
    return 0.5 * x * (1.0 + lax.erf(x * (0.5 ** 0.5)))


def _silu(x):
    half = 0.5 * x
    return half + half * jnp.tanh(half)


def _log_sigmoid(x):
    return jnp.minimum(x, 0.0) - jnp.log(1.0 + jnp.exp(-jnp.abs(x)))


def _rms_scale(x):
    return lax.rsqrt(jnp.mean(x * x, axis=-1, keepdims=True) + EPS)


PHASE_WEIGHTS = {
    "sgu_value": [(True, OFF_V, D_A)],
    "sgu_gate": [(True, OFF_U, D_A), (True, OFF_AG, D_A)],
    "gla_q": [(True, OFF_Q, D_QK)],
    "gla_k": [(True, OFF_K, D_QK)],
    "gla_v": [(True, OFF_BV, D_B)],
    "gla_gate": [(True, OFF_BG, D_B)],
    "out_a": [(False, 0, D_A)],
    "out_b": [(False, D_A, D_B)],
}


class _WeightLoader:
    def __init__(self, w_in_t_hbm, w_out_hbm, w_in_ref, w_out_ref, stage_ref, sem):
        self.refs = (w_in_t_hbm, w_out_hbm, w_in_ref, w_out_ref, stage_ref, sem)
        self.slots, self.rows = stage_ref.shape[:2]
        self.jobs, self.jobs_done_before = [], {}
        for name in PHASE_ORDER:
            for is_w_in, row0, n_rows in PHASE_WEIGHTS.get(name, ()):
                self.jobs += [(is_w_in, r0) for r0 in range(row0, row0 + n_rows, self.rows)]
            self.jobs_done_before[name] = len(self.jobs)
        self.done = 0
        for i in range(min(self.slots - 1, len(self.jobs))):
            self._copy(i).start()

    def _copy(self, i):
        w_in_t_hbm, w_out_hbm, _, _, stage_ref, sem = self.refs
        is_w_in, r0 = self.jobs[i]
        src = w_in_t_hbm if is_w_in else w_out_hbm
        return pltpu.make_async_copy(src.at[pl.ds(r0, self.rows), :],
                                     stage_ref.at[i % self.slots], sem.at[i % self.slots])

    def ensure(self, name):
        _, _, w_in_ref, w_out_ref, stage_ref, _ = self.refs
        for i in range(self.done, self.jobs_done_before[name]):
            if i + self.slots - 1 < len(self.jobs):
                self._copy(i + self.slots - 1).start()
            self._copy(i).wait()
            is_w_in, r0 = self.jobs[i]
            if is_w_in:
                w_in_ref[:, r0:r0 + self.rows] = stage_ref[i % self.slots].T.astype(BF16)
            else:
                w_out_ref[r0:r0 + self.rows, :] = stage_ref[i % self.slots].astype(BF16)
        self.done = max(self.done, self.jobs_done_before[name])


def _prepare_small_operands(w_in_t_hbm, w_a2_in_ref, bs_in_ref, w_lr_ref, w_a2_ref, bs_ref,
                            lr_stage_ref, sem):
    lr_copy = pltpu.make_async_copy(w_in_t_hbm.at[pl.ds(OFF_LR, B_GATE_RANK), :],
                                    lr_stage_ref.at[pl.ds(0, B_GATE_RANK), :], sem)
    lr_copy.start()
    lr_stage_ref[B_GATE_RANK:, :] = jnp.zeros((LR_PAD - B_GATE_RANK, D_MODEL), F32)
    w_a2_ref[...] = jnp.zeros_like(w_a2_ref)
    w_a2_ref[:B_GATE_RANK, :] = w_a2_in_ref[...].astype(BF16)
    row = lax.broadcasted_iota(jnp.int32, (A_CHUNK, A_CHUNK), 0)
    col = lax.broadcasted_iota(jnp.int32, (A_CHUNK, A_CHUNK), 1)
    ones = jnp.ones((A_CHUNK, A_CHUNK), BF16)
    for g in range(A_GROUPS):
        rest = jnp.where(row == col, bs_in_ref[g:g + 1, :], 0.0)
        acc = jnp.zeros((A_CHUNK, A_CHUNK), F32)
        for _ in range(3):
            part = rest.astype(BF16)
            acc = acc + jnp.dot(part, ones, preferred_element_type=F32)
            rest = rest - part.astype(F32)
        bs_ref[g] = acc
    lr_copy.wait()
    w_lr_ref[...] = lr_stage_ref[...].T.astype(BF16)


def _layer_kernel(x_ref, pre_g_ref, w_in_hbm, w_a2_in_ref, b_a2_ref, ln_g_ref, ln_b_ref,
                  ws_ref, bs_in_ref, bn_g_ref, w_out_hbm, post_g_ref, o_ref,
                  w_in_ref, w_out_ref, stage_ref, stage_sem,
                  w_lr_ref, w_a2_ref, bs_ref, lr_stage_ref, lr_sem,
                  h_ref, vn_ref, ug_ref, qa_ref, qb_ref, ka_ref, kb_ref, q2_ref, k2_ref, v_ref,
                  bg_ref, bc_ref, sprev_ref, mixed_ref, state_ref):
    tile = x_ref.shape[0]
    blocks = [slice(r0, r0 + GLA_BLOCK) for r0 in range(0, tile, GLA_BLOCK)]
    nt = (((1,), (1,)), ((), ()))
    tn = (((0,), (0,)), ((), ()))

    @pl.when(pl.program_id(1) == 0)
    def _reset_state():
        state_ref[...] = jnp.zeros_like(state_ref)

    def proj(off, width):
        return jnp.dot(h_ref[...], w_in_ref[:, off:off + width],
                       preferred_element_type=F32)

    val = {}

    def gate_lowrank():
        val["lr"] = jnp.dot(h_ref[...], w_lr_ref[...],
                            preferred_element_type=F32).astype(BF16)

    def gate_log_decay():
        gate_logit = (jnp.dot(val["lr"], w_a2_ref[...], preferred_element_type=F32)
                      + b_a2_ref[...])
        val["log_a"] = _log_sigmoid(gate_logit) * (1.0 / B_GATE_TAU)

    def gla_q():
        val["q"] = proj(OFF_Q, D_QK) * (B_DK ** -0.5)

    def gla_k():
        val["k"] = proj(OFF_K, D_QK)

    r_t = lax.broadcasted_iota(jnp.int32, (GLA_BLOCK, GLA_BLOCK), 0)
    c_t = lax.broadcasted_iota(jnp.int32, (GLA_BLOCK, GLA_BLOCK), 1)
    shift = B_CHUNK.bit_length() - 1
    same_chunk_causal = (r_t >= c_t) & ((r_t >> shift) == (c_t >> shift))
    pair_causal = r_t >= c_t

    def gla_cumsum():
        cum_mat = same_chunk_causal.astype(BF16)
        cum_mat2 = jnp.concatenate([cum_mat, cum_mat], axis=1)
        log_a = val["log_a"]
        la_hi = log_a.astype(BF16)
        la_lo = (log_a - la_hi.astype(F32)).astype(BF16)
        for blk in blocks:
            bc_ref[blk, :] = jnp.dot(cum_mat2, jnp.concatenate([la_hi[blk], la_lo[blk]], axis=0),
                                     preferred_element_type=F32)

    def gla_v():
        v_ref[...] = proj(OFF_BV, D_B).astype(BF16)

    def gla_operands():
        q, k = val["q"], val["k"]
        zeros = jnp.zeros((B_CHUNK, D_QK), BF16)
        pair_decays = []
        for blk in blocks:
            first = slice(blk.start, blk.start + B_CHUNK)
            second = slice(blk.start + B_CHUNK, blk.stop)
            b_1, b_2 = bc_ref[first, :], bc_ref[second, :]
            b_last_1 = bc_ref[first.stop - 1:first.stop, :]
            b_last_2 = bc_ref[second.stop - 1:second.stop, :]
            b_mid_1 = bc_ref[first.start + B_CHUNK // 2 - 1:first.start + B_CHUNK // 2, :]
            b_mid_2 = bc_ref[second.start + B_CHUNK // 2 - 1:second.start + B_CHUNK // 2, :]
            q_1, q_2, k_1, k_2 = q[first], q[second], k[first], k[second]
            dec_1, dec_2 = jnp.exp(b_last_1), jnp.exp(b_last_2)
            ke_2 = (k_2 * jnp.exp(b_mid_2 - b_2)).astype(BF16)
            qa_ref[first, :] = (q_1 * jnp.exp(b_1 - b_mid_1)).astype(BF16)
            qa_ref[second, :] = zeros
            qb_ref[first, :] = zeros
            qb_ref[second, :] = (q_2 * jnp.exp(b_2 - b_mid_2)).astype(BF16)
            ka_ref[first, :] = (k_1 * jnp.exp(b_mid_1 - b_1)).astype(BF16)
            ka_ref[second, :] = ke_2
            kb_ref[first, :] = (k_1 * jnp.exp(b_mid_2 + (b_last_1 - b_1))).astype(BF16)
            kb_ref[second, :] = ke_2
            q2_ref[first, :] = (q_1 * jnp.exp(b_1)).astype(BF16)
            q2_ref[second, :] = (q_2 * jnp.exp(b_2) * dec_1).astype(BF16)
            k2_ref[first, :] = (k_1 * jnp.exp(b_last_1 - b_1) * dec_2).astype(BF16)
            k2_ref[second, :] = (k_2 * jnp.exp(b_last_2 - b_2)).astype(BF16)
            pair_decays.append(dec_1 * dec_2)
        val["pair_decays"] = pair_decays

    def gla_gate():
        bg_ref[...] = _silu(proj(OFF_BG, D_B))

    def gla_state():
        for hd in range(B_HEADS):
            kc_ = slice(hd * B_DK, (hd + 1) * B_DK)
            vc_ = slice(hd * B_DV, (hd + 1) * B_DV)
            s_t = state_ref[hd]
            for p, blk in enumerate(blocks):
                sprev_ref[p, hd] = s_t.T.astype(BF16)
                kv_t = lax.dot_general(v_ref[blk, vc_], k2_ref[blk, kc_], tn,
                                       preferred_element_type=F32)
                s_t = s_t * val["pair_decays"][p][:, kc_] + kv_t
            state_ref[hd] = s_t

    def gla_scores():
        scores = {}
        for blk in blocks:
            for hd in range(B_HEADS):
                kc_ = slice(hd * B_DK, (hd + 1) * B_DK)
                q_ab = jnp.concatenate([qa_ref[blk, kc_], qb_ref[blk, kc_]], axis=1)
                k_ab = jnp.concatenate([ka_ref[blk, kc_], kb_ref[blk, kc_]], axis=1)
                s_hd = lax.dot_general(q_ab, k_ab, nt, preferred_element_type=F32)
                scores[blk.start, hd] = jnp.where(pair_causal, s_hd, 0.0).astype(BF16)
        val["scores"] = scores

    def gla_output():
        for p, blk in enumerate(blocks):
            for hd in range(B_HEADS):
                kc_ = slice(hd * B_DK, (hd + 1) * B_DK)
                vc_ = slice(hd * B_DV, (hd + 1) * B_DV)
                lhs = jnp.concatenate([val["scores"][blk.start, hd], q2_ref[blk, kc_]], axis=1)
                rhs = jnp.concatenate([v_ref[blk, vc_], sprev_ref[p, hd]], axis=0)
                o = jnp.dot(lhs, rhs, preferred_element_type=F32)
                o_n = o * _rms_scale(o) * bn_g_ref[:, vc_]
                mixed_ref[blk, D_A + hd * B_DV:D_A + (hd + 1) * B_DV] = (
                    o_n * bg_ref[blk, vc_]).astype(BF16)

    def sgu_value():
        v_a = _gelu(proj(OFF_V, D_A))
        mu = jnp.mean(v_a, axis=-1, keepdims=True)
        v_c = v_a - mu
        v_n = v_c * lax.rsqrt(jnp.mean(v_c * v_c, axis=-1, keepdims=True) + EPS)
        vn_ref[...] = (v_n * ln_g_ref[...] + ln_b_ref[...]).astype(BF16)

    def sgu_u():
        pass

    def sgu_gate():
        for c0 in range(0, D_A, COL_BLOCK):
            u_blk = _gelu(proj(OFF_U + c0, COL_BLOCK))
            g_blk = _silu(proj(OFF_AG + c0, COL_BLOCK))
            ug_ref[:, c0:c0 + COL_BLOCK] = u_blk * g_blk

    def sgu_mix():
        row = lax.broadcasted_iota(jnp.int32, (A_CHUNK, A_CHUNK), 0)
        col = lax.broadcasted_iota(jnp.int32, (A_CHUNK, A_CHUNK), 1)
        causal_a = row >= col
        for g in range(A_GROUPS):
            w_g = jnp.where(causal_a, ws_ref[g], 0.0).astype(BF16)
            gc = slice(g * A_GROUP_DIM, (g + 1) * A_GROUP_DIM)
            for n in range(tile // A_CHUNK):
                rs = slice(n * A_CHUNK, (n + 1) * A_CHUNK)
                sp = jnp.dot(w_g, vn_ref[rs, gc], preferred_element_type=F32) + bs_ref[g]
                mixed_ref[rs, gc] = (ug_ref[rs, gc] * sp).astype(BF16)

    def out_a():
        val["y_a"] = jnp.dot(mixed_ref[:, :D_A], w_out_ref[:D_A, :],
                             preferred_element_type=F32)

    def out_b():
        val["y_b"] = jnp.dot(mixed_ref[:, D_A:], w_out_ref[D_A:, :],
                             preferred_element_type=F32)

    phases = dict(
        gate_lowrank=gate_lowrank, gate_log_decay=gate_log_decay, gla_q=gla_q, gla_k=gla_k,
        gla_cumsum=gla_cumsum, gla_v=gla_v, gla_operands=gla_operands, gla_gate=gla_gate,
        gla_state=gla_state, gla_scores=gla_scores, gla_output=gla_output,
        sgu_value=sgu_value, sgu_u=sgu_u, sgu_gate=sgu_gate, sgu_mix=sgu_mix,
        out_a=out_a, out_b=out_b)
    assert sorted(PHASE_ORDER) == sorted(phases)

    def tile_step(loader):
        x = x_ref[...]
        h_ref[...] = (x * _rms_scale(x) * pre_g_ref[...]).astype(BF16)
        for name in PHASE_ORDER:
            if loader is not None:
                loader.ensure(name)
            phases[name]()
        y = val["y_a"] + val["y_b"]
        o_ref[...] = x_ref[...] + y * _rms_scale(y) * post_g_ref[...]

    is_first = (pl.program_id(0) == 0) & (pl.program_id(1) == 0)

    @pl.when(is_first)
    def _first_step():
        loader = _WeightLoader(w_in_hbm, w_out_hbm, w_in_ref, w_out_ref, stage_ref, stage_sem)
        _prepare_small_operands(w_in_hbm, w_a2_in_ref, bs_in_ref, w_lr_ref, w_a2_ref, bs_ref,
                                lr_stage_ref, lr_sem)
        tile_step(loader)
        assert loader.done == len(loader.jobs)

    @pl.when(jnp.logical_not(is_first))
    def _later_steps():
        tile_step(None)


def _resident(shape):
    zeros = (0,) * len(shape)
    return pl.BlockSpec(shape, lambda b, s: zeros, pipeline_mode=pl.Buffered(1))


def _hybrid_layer(x, pre_g, w_in, w_a2, b_a2, ln_g, ln_b, w_s, b_s, bn_g, w_out, post_g):
    bsz, seq, d = x.shape
    assert d == D_MODEL and seq % SEQ_TILE == 0
    assert SEQ_TILE % A_CHUNK == 0 and SEQ_TILE % GLA_BLOCK == 0 and GLA_BLOCK == 2 * B_CHUNK
    assert w_in.shape == (D_MODEL, D_IN) and w_out.shape == (D_MIX, D_MODEL)
    assert OFF_LR % WEIGHT_STAGE_ROWS == 0 and D_MIX % WEIGHT_STAGE_ROWS == 0

    w_in_t = jnp.swapaxes(w_in, 0, 1).astype(F32)
    row_vec = lambda a: a.reshape(1, -1).astype(F32)

    x_spec = pl.BlockSpec((None, SEQ_TILE, D_MODEL), lambda b, s: (b, s, 0))
    hbm_spec = pl.BlockSpec(memory_space=pl.ANY)
    n_pairs = SEQ_TILE // GLA_BLOCK
    scratch = [
        pltpu.VMEM((D_MODEL, OFF_LR), BF16),
        pltpu.VMEM((D_MIX, D_MODEL), BF16),
        pltpu.VMEM((WEIGHT_STAGE_SLOTS, WEIGHT_STAGE_ROWS, D_MODEL), F32),
        pltpu.SemaphoreType.DMA((WEIGHT_STAGE_SLOTS,)),
        pltpu.VMEM((D_MODEL, LR_PAD), BF16),
        pltpu.VMEM((LR_PAD, D_QK), BF16),
        pltpu.VMEM((A_GROUPS, A_CHUNK, A_CHUNK), F32),
        pltpu.VMEM((LR_PAD, D_MODEL), F32),
        pltpu.SemaphoreType.DMA(()),
        pltpu.VMEM((SEQ_TILE, D_MODEL), BF16),
        pltpu.VMEM((SEQ_TILE, D_A), BF16),
        pltpu.VMEM((SEQ_TILE, D_A), F32),
        pltpu.VMEM((SEQ_TILE, D_QK), BF16),
        pltpu.VMEM((SEQ_TILE, D_QK), BF16),
        pltpu.VMEM((SEQ_TILE, D_QK), BF16),
        pltpu.VMEM((SEQ_TILE, D_QK), BF16),
        pltpu.VMEM((SEQ_TILE, D_QK), BF16),
        pltpu.VMEM((SEQ_TILE, D_QK), BF16),
        pltpu.VMEM((SEQ_TILE, D_B), BF16),
        pltpu.VMEM((SEQ_TILE, D_B), F32),
        pltpu.VMEM((SEQ_TILE, D_QK), F32),
        pltpu.VMEM((n_pairs, B_HEADS, B_DK, B_DV), BF16),
        pltpu.VMEM((SEQ_TILE, D_MIX), BF16),
        pltpu.VMEM((B_HEADS, B_DV, B_DK), F32),
    ]
    return pl.pallas_call(
        _layer_kernel,
        out_shape=jax.ShapeDtypeStruct(x.shape, x.dtype),
        grid=(bsz, seq // SEQ_TILE),
        in_specs=[
            x_spec,
            _resident((1, D_MODEL)),
            hbm_spec,
            _resident((B_GATE_RANK, D_QK)),
            _resident((1, D_QK)),
            _resident((1, D_A)),
            _resident((1, D_A)),
            _resident((A_GROUPS, A_CHUNK, A_CHUNK)),
            _resident((A_GROUPS, A_CHUNK)),
            _resident((1, D_B)),
            hbm_spec,
            _resident((1, D_MODEL)),
        ],
        out_specs=x_spec,
        scratch_shapes=scratch,
        compiler_params=pltpu.CompilerParams(
            dimension_semantics=("arbitrary", "arbitrary"),
            vmem_limit_bytes=V7X_VMEM_LIMIT_BYTES),
        name="hybrid_gmlp_gla_layer",
    )(x, row_vec(pre_g), w_in_t, w_a2.astype(F32), row_vec(b_a2), row_vec(ln_g),
      row_vec(ln_b), w_s.astype(F32), b_s.astype(F32), row_vec(bn_g), w_out.astype(F32),
      row_vec(post_g))


def kernel(x, pre_norm_g, w_in, w_a2, b_a2, a_ln_g, a_ln_b, a_w_s, a_b_s, b_norm_g, w_out, post_norm_g):
    for l in range(pre_norm_g.shape[0]):
        x = _hybrid_layer(x, pre_norm_g[l], w_in[l], w_a2[l], b_a2[l], a_ln_g[l], a_ln_b[l],
                          a_w_s[l], a_b_s[l], b_norm_g[l], w_out[l], post_norm_g[l])
    return x
```

```python
import functools

import jax
import jax.numpy as jnp
from jax import lax
from jax.experimental import pallas as pl
from jax.experimental.pallas import tpu as pltpu

D_MODEL = 1024
D_MIX = 2 * D_MODEL
D_A = D_MIX // 2
D_B = D_MIX - D_A
A_GROUPS = 8
A_GROUP_DIM = D_A // A_GROUPS
A_CHUNK = 128
B_HEADS = 4
B_DK = D_B // 2 // B_HEADS
B_DV = D_B // B_HEADS
B_GATE_RANK = 16
B_GATE_TAU = 16.0
B_CHUNK = 64
D_QK = B_HEADS * B_DK
EPS = 1e-6

OFF_U = 0
OFF_V = OFF_U + D_A
OFF_AG = OFF_V + D_A
OFF_Q = OFF_AG + D_A
OFF_K = OFF_Q + D_QK
OFF_BV = OFF_K + D_QK
OFF_BG = OFF_BV + D_B
OFF_LR = OFF_BG + D_B
D_IN = OFF_LR + B_GATE_RANK

V7X_LANES = 128
LR_PAD = V7X_LANES

SEQ_TILE = 512
GLA_BLOCK = 2 * B_CHUNK
COL_BLOCK = 512
V7X_VMEM_LIMIT_BYTES = 56 * 1024 * 1024
WEIGHT_STAGE_ROWS = 256
WEIGHT_STAGE_SLOTS = 8

PHASE_ORDER = (
    "gate_lowrank", "sgu_value", "gate_log_decay", "sgu_u", "sgu_gate", "sgu_mix",
    "gla_q", "gla_k", "gla_cumsum", "out_a", "gla_v", "gla_operands", "gla_gate",
    "gla_state", "gla_scores", "gla_output", "out_b",
)

F32 = jnp.float32
BF16 = jnp.bfloat16


def _gelu(x):
    return 0.5 * x * (1.0 + lax.erf(x * (0.5 ** 0.5)))


def _silu(x):
    half = 0.5 * x
    return half + half * jnp.tanh(half)


def _log_sigmoid(x):
    return jnp.minimum(x, 0.0) - jnp.log(1.0 + jnp.exp(-jnp.abs(x)))


def _rms_scale(x):
    return lax.rsqrt(jnp.mean(x * x, axis=-1, keepdims=True) + EPS)


def _load_weights_as_bf16(w_in_t_hbm, w_out_hbm, w_in_ref, w_out_ref, stage_ref, sem,
                          while_first_tiles_arrive):
    slots, rows = stage_ref.shape[:2]

    def load(src_hbm, dst_ref, convert, while_first_tiles_arrive=lambda: None):
        n_jobs = dst_ref.shape[0]

        def copy(i):
            start = i * rows if isinstance(i, int) else pl.multiple_of(i * rows, rows)
            return pltpu.make_async_copy(src_hbm.at[pl.ds(start, rows), :],
                                         stage_ref.at[i % slots], sem.at[i % slots])

        for i in range(min(slots - 1, n_jobs)):
            copy(i).start()
        while_first_tiles_arrive()

        def body(i, carry):
            @pl.when(i + slots - 1 < n_jobs)
            def _refill():
                copy(i + slots - 1).start()
            copy(i).wait()
            dst_ref[i] = convert(stage_ref[i % slots])
            return carry

        lax.fori_loop(0, n_jobs, body, 0)

    load(w_in_t_hbm, w_in_ref, lambda tile: tile.T.astype(BF16), while_first_tiles_arrive)
    load(w_out_hbm, w_out_ref, lambda tile: tile.astype(BF16))


def _prepare_small_operands(w_in_t_hbm, w_a2_in_ref, bs_in_ref, w_lr_ref, w_a2_ref, bs_ref,
                            lr_stage_ref, sem):
    lr_copy = pltpu.make_async_copy(w_in_t_hbm.at[pl.ds(OFF_LR, B_GATE_RANK), :],
                                    lr_stage_ref.at[pl.ds(0, B_GATE_RANK), :], sem)
    lr_copy.start()
    lr_stage_ref[B_GATE_RANK:, :] = jnp.zeros((LR_PAD - B_GATE_RANK, D_MODEL), F32)
    w_a2_ref[...] = jnp.zeros_like(w_a2_ref)
    w_a2_ref[:B_GATE_RANK, :] = w_a2_in_ref[...].astype(BF16)
    row = lax.broadcasted_iota(jnp.int32, (A_CHUNK, A_CHUNK), 0)
    col = lax.broadcasted_iota(jnp.int32, (A_CHUNK, A_CHUNK), 1)
    ones = jnp.ones((A_CHUNK, A_CHUNK), BF16)
    for g in range(A_GROUPS):
        rest = jnp.where(row == col, bs_in_ref[g:g + 1, :], 0.0)
        acc = jnp.zeros((A_CHUNK, A_CHUNK), F32)
        for _ in range(3):
            part = rest.astype(BF16)
            acc = acc + jnp.dot(part, ones, preferred_element_type=F32)
            rest = rest - part.astype(F32)
        bs_ref[g] = acc
    lr_copy.wait()
    w_lr_ref[...] = lr_stage_ref[...].T.astype(BF16)


def _layer_kernel(x_ref, pre_g_ref, w_in_hbm, w_a2_in_ref, b_a2_ref, ln_g_ref, ln_b_ref,
                  ws_ref, bs_in_ref, bn_g_ref, w_out_hbm, post_g_ref, o_ref,
                  w_in_ref, w_out_ref, stage_ref, stage_sem,
                  w_lr_ref, w_a2_ref, bs_ref, lr_stage_ref, lr_sem,
                  h_ref, vn_ref, ug_ref, qa_ref, qb_ref, ka_ref, kb_ref, q2_ref, k2_ref, v_ref,
                  bg_ref, bc_ref, sprev_ref, mixed_ref, state_ref):
    tile = x_ref.shape[0]
    blocks = [slice(r0, r0 + GLA_BLOCK) for r0 in range(0, tile, GLA_BLOCK)]
    nt = (((1,), (1,)), ((), ()))
    tn = (((0,), (0,)), ((), ()))

    @pl.when(pl.program_id(1) == 0)
    def _reset_state():
        state_ref[...] = jnp.zeros_like(state_ref)

    def w_out_rows(r0, r1):
        return jnp.concatenate([w_out_ref[j] for j in range(r0 // WEIGHT_STAGE_ROWS,
                                                            r1 // WEIGHT_STAGE_ROWS)], axis=0)

    def proj(off, width):
        w = jnp.concatenate([w_in_ref[j] for j in range(off // WEIGHT_STAGE_ROWS,
                                                        (off + width) // WEIGHT_STAGE_ROWS)], axis=1)
        return jnp.dot(h_ref[...], w, preferred_element_type=F32)

    val = {}

    def gate_lowrank():
        val["lr"] = jnp.dot(h_ref[...], w_lr_ref[...],
                            preferred_element_type=F32).astype(BF16)

    def gate_log_decay():
        gate_logit = (jnp.dot(val["lr"], w_a2_ref[...], preferred_element_type=F32)
                      + b_a2_ref[...])
        val["log_a"] = _log_sigmoid(gate_logit) * (1.0 / B_GATE_TAU)

    def gla_q():
        val["q"] = proj(OFF_Q, D_QK) * (B_DK ** -0.5)

    def gla_k():
        val["k"] = proj(OFF_K, D_QK)

    r_t = lax.broadcasted_iota(jnp.int32, (GLA_BLOCK, GLA_BLOCK), 0)
    c_t = lax.broadcasted_iota(jnp.int32, (GLA_BLOCK, GLA_BLOCK), 1)
    shift = B_CHUNK.bit_length() - 1
    same_chunk_causal = (r_t >= c_t) & ((r_t >> shift) == (c_t >> shift))
    pair_causal = r_t >= c_t

    def gla_cumsum():
        cum_mat = same_chunk_causal.astype(BF16)
        cum_mat2 = jnp.concatenate([cum_mat, cum_mat], axis=1)
        log_a = val["log_a"]
        la_hi = log_a.astype(BF16)
        la_lo = (log_a - la_hi.astype(F32)).astype(BF16)
        for blk in blocks:
            bc_ref[blk, :] = jnp.dot(cum_mat2, jnp.concatenate([la_hi[blk], la_lo[blk]], axis=0),
                                     preferred_element_type=F32)

    def gla_v():
        v_ref[...] = proj(OFF_BV, D_B).astype(BF16)

    def gla_operands():
        q, k = val["q"], val["k"]
        zeros = jnp.zeros((B_CHUNK, D_QK), BF16)
        pair_decays = []
        for blk in blocks:
            first = slice(blk.start, blk.start + B_CHUNK)
            second = slice(blk.start + B_CHUNK, blk.stop)
            b_1, b_2 = bc_ref[first, :], bc_ref[second, :]
            b_last_1 = bc_ref[first.stop - 1:first.stop, :]
            b_last_2 = bc_ref[second.stop - 1:second.stop, :]
            b_mid_1 = bc_ref[first.start + B_CHUNK // 2 - 1:first.start + B_CHUNK // 2, :]
            b_mid_2 = bc_ref[second.start + B_CHUNK // 2 - 1:second.start + B_CHUNK // 2, :]
            q_1, q_2, k_1, k_2 = q[first], q[second], k[first], k[second]
            dec_1, dec_2 = jnp.exp(b_last_1), jnp.exp(b_last_2)
            ke_2 = (k_2 * jnp.exp(b_mid_2 - b_2)).astype(BF16)
            qa_ref[first, :] = (q_1 * jnp.exp(b_1 - b_mid_1)).astype(BF16)
            qa_ref[second, :] = zeros
            qb_ref[first, :] = zeros
            qb_ref[second, :] = (q_2 * jnp.exp(b_2 - b_mid_2)).astype(BF16)
            ka_ref[first, :] = (k_1 * jnp.exp(b_mid_1 - b_1)).astype(BF16)
            ka_ref[second, :] = ke_2
            kb_ref[first, :] = (k_1 * jnp.exp(b_mid_2 + (b_last_1 - b_1))).astype(BF16)
            kb_ref[second, :] = ke_2
            q2_ref[first, :] = (q_1 * jnp.exp(b_1)).astype(BF16)
            q2_ref[second, :] = (q_2 * jnp.exp(b_2) * dec_1).astype(BF16)
            k2_ref[first, :] = (k_1 * jnp.exp(b_last_1 - b_1) * dec_2).astype(BF16)
            k2_ref[second, :] = (k_2 * jnp.exp(b_last_2 - b_2)).astype(BF16)
            pair_decays.append(dec_1 * dec_2)
        val["pair_decays"] = pair_decays

    def gla_gate():
        bg_ref[...] = _silu(proj(OFF_BG, D_B))

    def gla_state():
        for hd in range(B_HEADS):
            kc_ = slice(hd * B_DK, (hd + 1) * B_DK)
            vc_ = slice(hd * B_DV, (hd + 1) * B_DV)
            s_t = state_ref[hd]
            for p, blk in enumerate(blocks):
                sprev_ref[p, hd] = s_t.T.astype(BF16)
                kv_t = lax.dot_general(v_ref[blk, vc_], k2_ref[blk, kc_], tn,
                                       preferred_element_type=F32)
                s_t = s_t * val["pair_decays"][p][:, kc_] + kv_t
            state_ref[hd] = s_t

    def gla_scores():
        scores = {}
        for blk in blocks:
            for hd in range(B_HEADS):
                kc_ = slice(hd * B_DK, (hd + 1) * B_DK)
                q_ab = jnp.concatenate([qa_ref[blk, kc_], qb_ref[blk, kc_]], axis=1)
                k_ab = jnp.concatenate([ka_ref[blk, kc_], kb_ref[blk, kc_]], axis=1)
                s_hd = lax.dot_general(q_ab, k_ab, nt, preferred_element_type=F32)
                scores[blk.start, hd] = jnp.where(pair_causal, s_hd, 0.0).astype(BF16)
        val["scores"] = scores

    def gla_output():
        for p, blk in enumerate(blocks):
            for hd in range(B_HEADS):
                kc_ = slice(hd * B_DK, (hd + 1) * B_DK)
                vc_ = slice(hd * B_DV, (hd + 1) * B_DV)
                lhs = jnp.concatenate([val["scores"][blk.start, hd], q2_ref[blk, kc_]], axis=1)
                rhs = jnp.concatenate([v_ref[blk, vc_], sprev_ref[p, hd]], axis=0)
                o = jnp.dot(lhs, rhs, preferred_element_type=F32)
                o_n = o * _rms_scale(o) * bn_g_ref[:, vc_]
                mixed_ref[blk, D_A + hd * B_DV:D_A + (hd + 1) * B_DV] = (
                    o_n * bg_ref[blk, vc_]).astype(BF16)

    def sgu_value():
        v_a = _gelu(proj(OFF_V, D_A))
        mu = jnp.mean(v_a, axis=-1, keepdims=True)
        v_c = v_a - mu
        v_n = v_c * lax.rsqrt(jnp.mean(v_c * v_c, axis=-1, keepdims=True) + EPS)
        vn_ref[...] = (v_n * ln_g_ref[...] + ln_b_ref[...]).astype(BF16)

    def sgu_u():
        pass

    def sgu_gate():
        for c0 in range(0, D_A, COL_BLOCK):
            u_blk = _gelu(proj(OFF_U + c0, COL_BLOCK))
            g_blk = _silu(proj(OFF_AG + c0, COL_BLOCK))
            ug_ref[:, c0:c0 + COL_BLOCK] = u_blk * g_blk

    def sgu_mix():
        row = lax.broadcasted_iota(jnp.int32, (A_CHUNK, A_CHUNK), 0)
        col = lax.broadcasted_iota(jnp.int32, (A_CHUNK, A_CHUNK), 1)
        causal_a = row >= col
        for g in range(A_GROUPS):
            w_g = jnp.where(causal_a, ws_ref[g], 0.0).astype(BF16)
            gc = slice(g * A_GROUP_DIM, (g + 1) * A_GROUP_DIM)
            for n in range(tile // A_CHUNK):
                rs = slice(n * A_CHUNK, (n + 1) * A_CHUNK)
                sp = jnp.dot(w_g, vn_ref[rs, gc], preferred_element_type=F32) + bs_ref[g]
                mixed_ref[rs, gc] = (ug_ref[rs, gc] * sp).astype(BF16)

    def out_a():
        val["y_a"] = jnp.dot(mixed_ref[:, :D_A], w_out_rows(0, D_A),
                             preferred_element_type=F32)

    def out_b():
        val["y_b"] = jnp.dot(mixed_ref[:, D_A:], w_out_rows(D_A, D_MIX),
                             preferred_element_type=F32)

    phases = dict(
        gate_lowrank=gate_lowrank, gate_log_decay=gate_log_decay, gla_q=gla_q, gla_k=gla_k,
        gla_cumsum=gla_cumsum, gla_v=gla_v, gla_operands=gla_operands, gla_gate=gla_gate,
        gla_state=gla_state, gla_scores=gla_scores, gla_output=gla_output,
        sgu_value=sgu_value, sgu_u=sgu_u, sgu_gate=sgu_gate, sgu_mix=sgu_mix,
        out_a=out_a, out_b=out_b)
    assert sorted(PHASE_ORDER) == sorted(phases)

    @pl.when((pl.program_id(0) == 0) & (pl.program_id(1) == 0))
    def _first_step():
        _load_weights_as_bf16(
            w_in_hbm, w_out_hbm, w_in_ref, w_out_ref, stage_ref, stage_sem,
            functools.partial(_prepare_small_operands, w_in_hbm, w_a2_in_ref, bs_in_ref,
                              w_lr_ref, w_a2_ref, bs_ref, lr_stage_ref, lr_sem))

    x = x_ref[...]
    h_ref[...] = (x * _rms_scale(x) * pre_g_ref[...]).astype(BF16)
    for name in PHASE_ORDER:
        phases[name]()
    y = val["y_a"] + val["y_b"]
    o_ref[...] = x_ref[...] + y * _rms_scale(y) * post_g_ref[...]


def _resident(shape):
    zeros = (0,) * len(shape)
    return pl.BlockSpec(shape, lambda b, s: zeros, pipeline_mode=pl.Buffered(1))


def _hybrid_layer(x, pre_g, w_in, w_a2, b_a2, ln_g, ln_b, w_s, b_s, bn_g, w_out, post_g):
    bsz, seq, d = x.shape
    assert d == D_MODEL and seq % SEQ_TILE == 0
    assert SEQ_TILE % A_CHUNK == 0 and SEQ_TILE % GLA_BLOCK == 0 and GLA_BLOCK == 2 * B_CHUNK
    assert w_in.shape == (D_MODEL, D_IN) and w_out.shape == (D_MIX, D_MODEL)
    assert OFF_LR % WEIGHT_STAGE_ROWS == 0 and D_MIX % WEIGHT_STAGE_ROWS == 0

    w_in_t = jnp.swapaxes(w_in, 0, 1).astype(F32)
    row_vec = lambda a: a.reshape(1, -1).astype(F32)

    x_spec = pl.BlockSpec((None, SEQ_TILE, D_MODEL), lambda b, s: (b, s, 0))
    hbm_spec = pl.BlockSpec(memory_space=pl.ANY)
    n_pairs = SEQ_TILE // GLA_BLOCK
    scratch = [
        pltpu.VMEM((OFF_LR // WEIGHT_STAGE_ROWS, D_MODEL, WEIGHT_STAGE_ROWS), BF16),
        pltpu.VMEM((D_MIX // WEIGHT_STAGE_ROWS, WEIGHT_STAGE_ROWS, D_MODEL), BF16),
        pltpu.VMEM((WEIGHT_STAGE_SLOTS, WEIGHT_STAGE_ROWS, D_MODEL), F32),
        pltpu.SemaphoreType.DMA((WEIGHT_STAGE_SLOTS,)),
        pltpu.VMEM((D_MODEL, LR_PAD), BF16),
        pltpu.VMEM((LR_PAD, D_QK), BF16),
        pltpu.VMEM((A_GROUPS, A_CHUNK, A_CHUNK), F32),
        pltpu.VMEM((LR_PAD, D_MODEL), F32),
        pltpu.SemaphoreType.DMA(()),
        pltpu.VMEM((SEQ_TILE, D_MODEL), BF16),
        pltpu.VMEM((SEQ_TILE, D_A), BF16),
        pltpu.VMEM((SEQ_TILE, D_A), F32),
        pltpu.VMEM((SEQ_TILE, D_QK), BF16),
        pltpu.VMEM((SEQ_TILE, D_QK), BF16),
        pltpu.VMEM((SEQ_TILE, D_QK), BF16),
        pltpu.VMEM((SEQ_TILE, D_QK), BF16),
        pltpu.VMEM((SEQ_TILE, D_QK), BF16),
        pltpu.VMEM((SEQ_TILE, D_QK), BF16),
        pltpu.VMEM((SEQ_TILE, D_B), BF16),
        pltpu.VMEM((SEQ_TILE, D_B), F32),
        pltpu.VMEM((SEQ_TILE, D_QK), F32),
        pltpu.VMEM((n_pairs, B_HEADS, B_DK, B_DV), BF16),
        pltpu.VMEM((SEQ_TILE, D_MIX), BF16),
        pltpu.VMEM((B_HEADS, B_DV, B_DK), F32),
    ]
    return pl.pallas_call(
        _layer_kernel,
        out_shape=jax.ShapeDtypeStruct(x.shape, x.dtype),
        grid=(bsz, seq // SEQ_TILE),
        in_specs=[
            x_spec,
            _resident((1, D_MODEL)),
            hbm_spec,
            _resident((B_GATE_RANK, D_QK)),
            _resident((1, D_QK)),
            _resident((1, D_A)),
            _resident((1, D_A)),
            _resident((A_GROUPS, A_CHUNK, A_CHUNK)),
            _resident((A_GROUPS, A_CHUNK)),
            _resident((1, D_B)),
            hbm_spec,
            _resident((1, D_MODEL)),
        ],
        out_specs=x_spec,
        scratch_shapes=scratch,
        compiler_params=pltpu.CompilerParams(
            dimension_semantics=("arbitrary", "arbitrary"),
            vmem_limit_bytes=V7X_VMEM_LIMIT_BYTES),
        name="hybrid_gmlp_gla_layer",
    )(x, row_vec(pre_g), w_in_t, w_a2.astype(F32), row_vec(b_a2), row_vec(ln_g),
      row_vec(ln_b), w_s.astype(F32), b_s.astype(F32), row_vec(bn_g), w_out.astype(F32),
      row_vec(post_g))


def kernel(x, pre_norm_g, w_in, w_a2, b_a2, a_ln_g, a_ln_b, a_w_s, a_b_s, b_norm_g, w_out, post_norm_g):
    for l in range(pre_norm_g.shape[0]):
        x = _hybrid_layer(x, pre_norm_g[l], w_in[l], w_a2[l], b_a2[l], a_ln_g[l], a_ln_b[l],
                          a_w_s[l], a_b_s[l], b_norm_g[l], w_out[l], post_norm_g[l])
    return x
```

```python
import functools

import jax
import jax.numpy as jnp
from jax import lax
from jax.experimental import pallas as pl
from jax.experimental.pallas import tpu as pltpu

D_MODEL = 1024
D_MIX = 2 * D_MODEL
D_A = D_MIX // 2
D_B = D_MIX - D_A
A_GROUPS = 8
A_GROUP_DIM = D_A // A_GROUPS
A_CHUNK = 128
B_HEADS = 4
B_DK = D_B // 2 // B_HEADS
B_DV = D_B // B_HEADS
B_GATE_RANK = 16
B_GATE_TAU = 16.0
B_CHUNK = 64
D_QK = B_HEADS * B_DK
EPS = 1e-6

OFF_U = 0
OFF_V = OFF_U + D_A
OFF_AG = OFF_V + D_A
OFF_Q = OFF_AG + D_A
OFF_K = OFF_Q + D_QK
OFF_BV = OFF_K + D_QK
OFF_BG = OFF_BV + D_B
OFF_LR = OFF_BG + D_B
D_IN = OFF_LR + B_GATE_RANK

V7X_LANES = 128
LR_PAD = V7X_LANES

SEQ_TILE = 512
GLA_BLOCK = 2 * B_CHUNK
COL_BLOCK = 512
V7X_VMEM_LIMIT_BYTES = 56 * 1024 * 1024
WEIGHT_STAGE_ROWS = 256
WEIGHT_STAGE_SLOTS = 8

PHASE_ORDER = (
    "gate_lowrank", "sgu_value", "gate_log_decay", "sgu_u", "sgu_gate", "sgu_mix",
    "gla_q", "gla_k", "gla_cumsum", "gla_v", "gla_operands", "gla_gate",
    "gla_state", "gla_scores", "gla_output", "out_proj",
)

F32 = jnp.float32
BF16 = jnp.bfloat16


def _gelu(x):
    return 0.5 * x * (1.0 + lax.erf(x * (0.5 ** 0.5)))


def _silu(x):
    half = 0.5 * x
    return half + half * jnp.tanh(half)


def _log_sigmoid(x):
    return jnp.minimum(x, 0.0) - jnp.log(1.0 + jnp.exp(-jnp.abs(x)))


def _rms_scale(x):
    return lax.rsqrt(jnp.mean(x * x, axis=-1, keepdims=True) + EPS)


def _load_weights_as_bf16(w_in_t_hbm, w_out_hbm, w_in_ref, w_out_ref, stage_ref, sem,
                          while_first_tiles_arrive):
    slots, rows = stage_ref.shape[:2]

    def load(src_hbm, dst_ref, convert, while_first_tiles_arrive=lambda: None):
        n_jobs = dst_ref.shape[0]

        def copy(i):
            start = i * rows if isinstance(i, int) else pl.multiple_of(i * rows, rows)
            return pltpu.make_async_copy(src_hbm.at[pl.ds(start, rows), :],
                                         stage_ref.at[i % slots], sem.at[i % slots])

        for i in range(min(slots - 1, n_jobs)):
            copy(i).start()
        while_first_tiles_arrive()

        def body(i, carry):
            @pl.when(i + slots - 1 < n_jobs)
            def _refill():
                copy(i + slots - 1).start()
            copy(i).wait()
            dst_ref[i] = convert(stage_ref[i % slots])
            return carry

        lax.fori_loop(0, n_jobs, body, 0)

    load(w_in_t_hbm, w_in_ref, lambda tile: tile.T.astype(BF16), while_first_tiles_arrive)
    load(w_out_hbm, w_out_ref, lambda tile: tile.astype(BF16))


def _prepare_small_operands(w_in_t_hbm, w_a2_in_ref, bs_in_ref, w_lr_ref, w_a2_ref, bs_ref,
                            lr_stage_ref, sem):
    lr_copy = pltpu.make_async_copy(w_in_t_hbm.at[pl.ds(OFF_LR, B_GATE_RANK), :],
                                    lr_stage_ref.at[pl.ds(0, B_GATE_RANK), :], sem)
    lr_copy.start()
    lr_stage_ref[B_GATE_RANK:, :] = jnp.zeros((LR_PAD - B_GATE_RANK, D_MODEL), F32)
    w_a2_ref[...] = jnp.zeros_like(w_a2_ref)
    w_a2_ref[:B_GATE_RANK, :] = w_a2_in_ref[...].astype(BF16)
    row = lax.broadcasted_iota(jnp.int32, (A_CHUNK, A_CHUNK), 0)
    col = lax.broadcasted_iota(jnp.int32, (A_CHUNK, A_CHUNK), 1)
    ones = jnp.ones((A_CHUNK, A_CHUNK), BF16)
    for g in range(A_GROUPS):
        rest = jnp.where(row == col, bs_in_ref[g:g + 1, :], 0.0)
        acc = jnp.zeros((A_CHUNK, A_CHUNK), F32)
        for _ in range(3):
            part = rest.astype(BF16)
            acc = acc + jnp.dot(part, ones, preferred_element_type=F32)
            rest = rest - part.astype(F32)
        bs_ref[g] = acc
    lr_copy.wait()
    w_lr_ref[...] = lr_stage_ref[...].T.astype(BF16)


def _layer_kernel(x_ref, pre_g_ref, w_in_hbm, w_a2_in_ref, b_a2_ref, ln_g_ref, ln_b_ref,
                  ws_ref, bs_in_ref, bn_g_ref, w_out_hbm, post_g_ref, o_ref,
                  w_in_ref, w_out_ref, stage_ref, stage_sem,
                  w_lr_ref, w_a2_ref, bs_ref, lr_stage_ref, lr_sem,
                  h_ref, vn_ref, ug_ref, qa_ref, qb_ref, ka_ref, kb_ref, q2_ref, k2_ref, v_ref,
                  bg_ref, bc_ref, sprev_ref, mixed_ref, state_ref):
    tile = x_ref.shape[0]
    blocks = [slice(r0, r0 + GLA_BLOCK) for r0 in range(0, tile, GLA_BLOCK)]
    nt = (((1,), (1,)), ((), ()))
    tn = (((0,), (0,)), ((), ()))

    @pl.when(pl.program_id(1) == 0)
    def _reset_state():
        state_ref[...] = jnp.zeros_like(state_ref)

    def proj(off, width):
        w = jnp.concatenate([w_in_ref[j] for j in range(off // WEIGHT_STAGE_ROWS,
                                                        (off + width) // WEIGHT_STAGE_ROWS)], axis=1)
        return jnp.dot(h_ref[...], w, preferred_element_type=F32)

    val = {}

    def gate_lowrank():
        val["lr"] = jnp.dot(h_ref[...], w_lr_ref[...],
                            preferred_element_type=F32).astype(BF16)

    def gate_log_decay():
        gate_logit = (jnp.dot(val["lr"], w_a2_ref[...], preferred_element_type=F32)
                      + b_a2_ref[...])
        val["log_a"] = _log_sigmoid(gate_logit) * (1.0 / B_GATE_TAU)

    def gla_q():
        val["q"] = proj(OFF_Q, D_QK) * (B_DK ** -0.5)

    def gla_k():
        val["k"] = proj(OFF_K, D_QK)

    r_t = lax.broadcasted_iota(jnp.int32, (GLA_BLOCK, GLA_BLOCK), 0)
    c_t = lax.broadcasted_iota(jnp.int32, (GLA_BLOCK, GLA_BLOCK), 1)
    shift = B_CHUNK.bit_length() - 1
    same_chunk_causal = (r_t >= c_t) & ((r_t >> shift) == (c_t >> shift))
    pair_causal = r_t >= c_t

    def gla_cumsum():
        cum_mat = same_chunk_causal.astype(BF16)
        cum_mat2 = jnp.concatenate([cum_mat, cum_mat], axis=1)
        log_a = val["log_a"]
        la_hi = log_a.astype(BF16)
        la_lo = (log_a - la_hi.astype(F32)).astype(BF16)
        for blk in blocks:
            bc_ref[blk, :] = jnp.dot(cum_mat2, jnp.concatenate([la_hi[blk], la_lo[blk]], axis=0),
                                     preferred_element_type=F32)

    def gla_v():
        v_ref[...] = proj(OFF_BV, D_B).astype(BF16)

    def gla_operands():
        q, k = val["q"], val["k"]
        zeros = jnp.zeros((B_CHUNK, D_QK), BF16)
        pair_decays = []
        for blk in blocks:
            first = slice(blk.start, blk.start + B_CHUNK)
            second = slice(blk.start + B_CHUNK, blk.stop)
            b_1, b_2 = bc_ref[first, :], bc_ref[second, :]
            b_last_1 = bc_ref[first.stop - 1:first.stop, :]
            b_last_2 = bc_ref[second.stop - 1:second.stop, :]
            b_mid_1 = bc_ref[first.start + B_CHUNK // 2 - 1:first.start + B_CHUNK // 2, :]
            b_mid_2 = bc_ref[second.start + B_CHUNK // 2 - 1:second.start + B_CHUNK // 2, :]
            q_1, q_2, k_1, k_2 = q[first], q[second], k[first], k[second]
            dec_1, dec_2 = jnp.exp(b_last_1), jnp.exp(b_last_2)
            ke_2 = (k_2 * jnp.exp(b_mid_2 - b_2)).astype(BF16)
            qa_ref[first, :] = (q_1 * jnp.exp(b_1 - b_mid_1)).astype(BF16)
            qa_ref[second, :] = zeros
            qb_ref[first, :] = zeros
            qb_ref[second, :] = (q_2 * jnp.exp(b_2 - b_mid_2)).astype(BF16)
            ka_ref[first, :] = (k_1 * jnp.exp(b_mid_1 - b_1)).astype(BF16)
            ka_ref[second, :] = ke_2
            kb_ref[first, :] = (k_1 * jnp.exp(b_mid_2 + (b_last_1 - b_1))).astype(BF16)
            kb_ref[second, :] = ke_2
            q2_ref[first, :] = (q_1 * jnp.exp(b_1)).astype(BF16)
            q2_ref[second, :] = (q_2 * jnp.exp(b_2) * dec_1).astype(BF16)
            k2_ref[first, :] = (k_1 * jnp.exp(b_last_1 - b_1) * dec_2).astype(BF16)
            k2_ref[second, :] = (k_2 * jnp.exp(b_last_2 - b_2)).astype(BF16)
            pair_decays.append(dec_1 * dec_2)
        val["pair_decays"] = pair_decays

    def gla_gate():
        bg_ref[...] = _silu(proj(OFF_BG, D_B))

    def gla_state():
        for hd in range(B_HEADS):
            kc_ = slice(hd * B_DK, (hd + 1) * B_DK)
            vc_ = slice(hd * B_DV, (hd + 1) * B_DV)
            s_t = state_ref[hd]
            for p, blk in enumerate(blocks):
                sprev_ref[p, hd] = s_t.T.astype(BF16)
                kv_t = lax.dot_general(v_ref[blk, vc_], k2_ref[blk, kc_], tn,
                                       preferred_element_type=F32)
                s_t = s_t * val["pair_decays"][p][:, kc_] + kv_t
            state_ref[hd] = s_t

    def gla_scores():
        scores = {}
        for blk in blocks:
            for hd in range(B_HEADS):
                kc_ = slice(hd * B_DK, (hd + 1) * B_DK)
                q_ab = jnp.concatenate([qa_ref[blk, kc_], qb_ref[blk, kc_]], axis=1)
                k_ab = jnp.concatenate([ka_ref[blk, kc_], kb_ref[blk, kc_]], axis=1)
                s_hd = lax.dot_general(q_ab, k_ab, nt, preferred_element_type=F32)
                scores[blk.start, hd] = jnp.where(pair_causal, s_hd, 0.0).astype(BF16)
        val["scores"] = scores

    def gla_output():
        for p, blk in enumerate(blocks):
            for hd in range(B_HEADS):
                kc_ = slice(hd * B_DK, (hd + 1) * B_DK)
                vc_ = slice(hd * B_DV, (hd + 1) * B_DV)
                lhs = jnp.concatenate([val["scores"][blk.start, hd], q2_ref[blk, kc_]], axis=1)
                rhs = jnp.concatenate([v_ref[blk, vc_], sprev_ref[p, hd]], axis=0)
                o = jnp.dot(lhs, rhs, preferred_element_type=F32)
                o_n = o * _rms_scale(o) * bn_g_ref[:, vc_]
                mixed_ref[blk, D_A + hd * B_DV:D_A + (hd + 1) * B_DV] = (
                    o_n * bg_ref[blk, vc_]).astype(BF16)

    def sgu_value():
        v_a = _gelu(proj(OFF_V, D_A))
        mu = jnp.mean(v_a, axis=-1, keepdims=True)
        v_c = v_a - mu
        v_n = v_c * lax.rsqrt(jnp.mean(v_c * v_c, axis=-1, keepdims=True) + EPS)
        vn_ref[...] = (v_n * ln_g_ref[...] + ln_b_ref[...]).astype(BF16)

    def sgu_u():
        pass

    def sgu_gate():
        for c0 in range(0, D_A, COL_BLOCK):
            u_blk = _gelu(proj(OFF_U + c0, COL_BLOCK))
            g_blk = _silu(proj(OFF_AG + c0, COL_BLOCK))
            ug_ref[:, c0:c0 + COL_BLOCK] = u_blk * g_blk

    def sgu_mix():
        row = lax.broadcasted_iota(jnp.int32, (A_CHUNK, A_CHUNK), 0)
        col = lax.broadcasted_iota(jnp.int32, (A_CHUNK, A_CHUNK), 1)
        causal_a = row >= col
        for g in range(A_GROUPS):
            w_g = jnp.where(causal_a, ws_ref[g], 0.0).astype(BF16)
            gc = slice(g * A_GROUP_DIM, (g + 1) * A_GROUP_DIM)
            for n in range(tile // A_CHUNK):
                rs = slice(n * A_CHUNK, (n + 1) * A_CHUNK)
                sp = jnp.dot(w_g, vn_ref[rs, gc], preferred_element_type=F32) + bs_ref[g]
                mixed_ref[rs, gc] = (ug_ref[rs, gc] * sp).astype(BF16)

    def out_proj():
        w_out = jnp.concatenate([w_out_ref[j] for j in range(w_out_ref.shape[0])], axis=0)
        val["y"] = jnp.dot(mixed_ref[...], w_out, preferred_element_type=F32)

    phases = dict(
        gate_lowrank=gate_lowrank, gate_log_decay=gate_log_decay, gla_q=gla_q, gla_k=gla_k,
        gla_cumsum=gla_cumsum, gla_v=gla_v, gla_operands=gla_operands, gla_gate=gla_gate,
        gla_state=gla_state, gla_scores=gla_scores, gla_output=gla_output,
        sgu_value=sgu_value, sgu_u=sgu_u, sgu_gate=sgu_gate, sgu_mix=sgu_mix,
        out_proj=out_proj)
    assert sorted(PHASE_ORDER) == sorted(phases)

    @pl.when((pl.program_id(0) == 0) & (pl.program_id(1) == 0))
    def _first_step():
        _load_weights_as_bf16(
            w_in_hbm, w_out_hbm, w_in_ref, w_out_ref, stage_ref, stage_sem,
            functools.partial(_prepare_small_operands, w_in_hbm, w_a2_in_ref, bs_in_ref,
                              w_lr_ref, w_a2_ref, bs_ref, lr_stage_ref, lr_sem))

    x = x_ref[...]
    h_ref[...] = (x * _rms_scale(x) * pre_g_ref[...]).astype(BF16)
    for name in PHASE_ORDER:
        phases[name]()
    y = val["y"]
    o_ref[...] = x_ref[...] + y * _rms_scale(y) * post_g_ref[...]


def _resident(shape):
    zeros = (0,) * len(shape)
    return pl.BlockSpec(shape, lambda b, s: zeros, pipeline_mode=pl.Buffered(1))


def _hybrid_layer(x, pre_g, w_in, w_a2, b_a2, ln_g, ln_b, w_s, b_s, bn_g, w_out, post_g):
    bsz, seq, d = x.shape
    assert d == D_MODEL and seq % SEQ_TILE == 0
    assert SEQ_TILE % A_CHUNK == 0 and SEQ_TILE % GLA_BLOCK == 0 and GLA_BLOCK == 2 * B_CHUNK
    assert w_in.shape == (D_MODEL, D_IN) and w_out.shape == (D_MIX, D_MODEL)
    assert OFF_LR % WEIGHT_STAGE_ROWS == 0 and D_MIX % WEIGHT_STAGE_ROWS == 0

    w_in_t = jnp.swapaxes(w_in, 0, 1).astype(F32)
    row_vec = lambda a: a.reshape(1, -1).astype(F32)

    x_spec = pl.BlockSpec((None, SEQ_TILE, D_MODEL), lambda b, s: (b, s, 0))
    hbm_spec = pl.BlockSpec(memory_space=pl.ANY)
    n_pairs = SEQ_TILE // GLA_BLOCK
    scratch = [
        pltpu.VMEM((OFF_LR // WEIGHT_STAGE_ROWS, D_MODEL, WEIGHT_STAGE_ROWS), BF16),
        pltpu.VMEM((D_MIX // WEIGHT_STAGE_ROWS, WEIGHT_STAGE_ROWS, D_MODEL), BF16),
        pltpu.VMEM((WEIGHT_STAGE_SLOTS, WEIGHT_STAGE_ROWS, D_MODEL), F32),
        pltpu.SemaphoreType.DMA((WEIGHT_STAGE_SLOTS,)),
        pltpu.VMEM((D_MODEL, LR_PAD), BF16),
        pltpu.VMEM((LR_PAD, D_QK), BF16),
        pltpu.VMEM((A_GROUPS, A_CHUNK, A_CHUNK), F32),
        pltpu.VMEM((LR_PAD, D_MODEL), F32),
        pltpu.SemaphoreType.DMA(()),
        pltpu.VMEM((SEQ_TILE, D_MODEL), BF16),
        pltpu.VMEM((SEQ_TILE, D_A), BF16),
        pltpu.VMEM((SEQ_TILE, D_A), F32),
        pltpu.VMEM((SEQ_TILE, D_QK), BF16),
        pltpu.VMEM((SEQ_TILE, D_QK), BF16),
        pltpu.VMEM((SEQ_TILE, D_QK), BF16),
        pltpu.VMEM((SEQ_TILE, D_QK), BF16),
        pltpu.VMEM((SEQ_TILE, D_QK), BF16),
        pltpu.VMEM((SEQ_TILE, D_QK), BF16),
        pltpu.VMEM((SEQ_TILE, D_B), BF16),
        pltpu.VMEM((SEQ_TILE, D_B), F32),
        pltpu.VMEM((SEQ_TILE, D_QK), F32),
        pltpu.VMEM((n_pairs, B_HEADS, B_DK, B_DV), BF16),
        pltpu.VMEM((SEQ_TILE, D_MIX), BF16),
        pltpu.VMEM((B_HEADS, B_DV, B_DK), F32),
    ]
    return pl.pallas_call(
        _layer_kernel,
        out_shape=jax.ShapeDtypeStruct(x.shape, x.dtype),
        grid=(bsz, seq // SEQ_TILE),
        in_specs=[
            x_spec,
            _resident((1, D_MODEL)),
            hbm_spec,
            _resident((B_GATE_RANK, D_QK)),
            _resident((1, D_QK)),
            _resident((1, D_A)),
            _resident((1, D_A)),
            _resident((A_GROUPS, A_CHUNK, A_CHUNK)),
            _resident((A_GROUPS, A_CHUNK)),
            _resident((1, D_B)),
            hbm_spec,
            _resident((1, D_MODEL)),
        ],
        out_specs=x_spec,
        scratch_shapes=scratch,
        compiler_params=pltpu.CompilerParams(
            dimension_semantics=("arbitrary", "arbitrary"),
            vmem_limit_bytes=V7X_VMEM_LIMIT_BYTES),
        name="hybrid_gmlp_gla_layer",
    )(x, row_vec(pre_g), w_in_t, w_a2.astype(F32), row_vec(b_a2), row_vec(ln_g),
      row_vec(ln_b), w_s.astype(F32), b_s.astype(F32), row_vec(bn_g), w_out.astype(F32),
      row_vec(post_g))


def kernel(x, pre_norm_g, w_in, w_a2, b_a2, a_ln_g, a_ln_b, a_w_s, a_b_s, b_norm_g, w_out, post_norm_g):
    for l in range(pre_norm_g.shape[0]):
        x = _hybrid_layer(x, pre_norm_g[l], w_in[l], w_a2[l], b_a2[l], a_ln_g[l], a_ln_b[l],
                          a_w_s[l], a_b_s[l], b_norm_g[l], w_out[l], post_norm_g[l])
    return x
```

```python
import functools

import jax
import jax.numpy as jnp
from jax import lax
from jax.experimental import pallas as pl
from jax.experimental.pallas import tpu as pltpu

D_MODEL = 1024
D_MIX = 2 * D_MODEL
D_A = D_MIX // 2
D_B = D_MIX - D_A
A_GROUPS = 8
A_GROUP_DIM = D_A // A_GROUPS
A_CHUNK = 128
B_HEADS = 4
B_DK = D_B // 2 // B_HEADS
B_DV = D_B // B_HEADS
B_GATE_RANK = 16
B_GATE_TAU = 16.0
B_CHUNK = 64
D_QK = B_HEADS * B_DK
EPS = 1e-6

OFF_U = 0
OFF_V = OFF_U + D_A
OFF_AG = OFF_V + D_A
OFF_Q = OFF_AG + D_A
OFF_K = OFF_Q + D_QK
OFF_BV = OFF_K + D_QK
OFF_BG = OFF_BV + D_B
OFF_LR = OFF_BG + D_B
D_IN = OFF_LR + B_GATE_RANK

V7X_LANES = 128
LR_PAD = V7X_LANES

SEQ_TILE = 512
GLA_BLOCK = 2 * B_CHUNK
COL_BLOCK = 512
V7X_VMEM_LIMIT_BYTES = 56 * 1024 * 1024
WEIGHT_STAGE_ROWS = 256
WEIGHT_STAGE_SLOTS = 8

PHASE_ORDER = (
    "gate_lowrank", "sgu_value", "gate_log_decay", "sgu_u", "sgu_gate", "sgu_mix",
    "gla_q", "gla_k", "gla_cumsum", "out_a", "gla_v", "gla_operands", "gla_gate",
    "gla_state", "gla_scores", "gla_output", "out_b",
)

F32 = jnp.float32
BF16 = jnp.bfloat16


def _gelu(x):
    return 0.5 * x * (1.0 + lax.erf(x * (0.5 ** 0.5)))


def _silu(x):
    half = 0.5 * x
    return half + half * jnp.tanh(half)


def _log_sigmoid(x):
    return jnp.minimum(x, 0.0) - jnp.log(1.0 + jnp.exp(-jnp.abs(x)))


def _rms_scale(x):
    return lax.rsqrt(jnp.mean(x * x, axis=-1, keepdims=True) + EPS)


def _load_weights_as_bf16(w_in_t_hbm, w_out_hbm, w_in_ref, w_out_ref, stage_ref, sem,
                          while_first_tiles_arrive):
    slots, rows = stage_ref.shape[:2]

    def load(src_hbm, dst_ref, convert, while_first_tiles_arrive=lambda: None):
        n_jobs = dst_ref.shape[0]

        def copy(i):
            start = i * rows if isinstance(i, int) else pl.multiple_of(i * rows, rows)
            return pltpu.make_async_copy(src_hbm.at[pl.ds(start, rows), :],
                                         stage_ref.at[i % slots], sem.at[i % slots])

        for i in range(min(slots - 1, n_jobs)):
            copy(i).start()
        while_first_tiles_arrive()

        def body(i, carry):
            @pl.when(i + slots - 1 < n_jobs)
            def _refill():
                copy(i + slots - 1).start()
            copy(i).wait()
            dst_ref[i] = convert(stage_ref[i % slots])
            return carry

        lax.fori_loop(0, n_jobs, body, 0)

    load(w_in_t_hbm, w_in_ref, lambda tile: tile.T.astype(BF16), while_first_tiles_arrive)
    load(w_out_hbm, w_out_ref, lambda tile: tile.astype(BF16))


def _prepare_small_operands(w_in_t_hbm, w_a2_in_ref, bs_in_ref, w_lr_ref, w_a2_ref, bs_ref,
                            lr_stage_ref, sem):
    lr_copy = pltpu.make_async_copy(w_in_t_hbm.at[pl.ds(OFF_LR, B_GATE_RANK), :],
                                    lr_stage_ref.at[pl.ds(0, B_GATE_RANK), :], sem)
    lr_copy.start()
    lr_stage_ref[B_GATE_RANK:, :] = jnp.zeros((LR_PAD - B_GATE_RANK, D_MODEL), F32)
    w_a2_ref[...] = jnp.zeros_like(w_a2_ref)
    w_a2_ref[:B_GATE_RANK, :] = w_a2_in_ref[...].astype(BF16)
    row = lax.broadcasted_iota(jnp.int32, (A_CHUNK, A_CHUNK), 0)
    col = lax.broadcasted_iota(jnp.int32, (A_CHUNK, A_CHUNK), 1)
    ones = jnp.ones((A_CHUNK, A_CHUNK), BF16)
    for g in range(A_GROUPS):
        rest = jnp.where(row == col, bs_in_ref[g:g + 1, :], 0.0)
        acc = jnp.zeros((A_CHUNK, A_CHUNK), F32)
        for _ in range(3):
            part = rest.astype(BF16)
            acc = acc + jnp.dot(part, ones, preferred_element_type=F32)
            rest = rest - part.astype(F32)
        bs_ref[g] = acc
    lr_copy.wait()
    w_lr_ref[...] = lr_stage_ref[...].T.astype(BF16)


def _layer_kernel(x_hbm, pre_g_ref, w_in_hbm, w_a2_in_ref, b_a2_ref, ln_g_ref, ln_b_ref,
                  ws_ref, bs_in_ref, bn_g_ref, w_out_hbm, post_g_ref, o_hbm,
                  w_in_ref, w_out_ref, stage_ref, stage_sem,
                  w_lr_ref, w_a2_ref, bs_ref, lr_stage_ref, lr_sem, *tile_scratch,
                  tiles_per_seq):
    _load_weights_as_bf16(
        w_in_hbm, w_out_hbm, w_in_ref, w_out_ref, stage_ref, stage_sem,
        functools.partial(_prepare_small_operands, w_in_hbm, w_a2_in_ref, bs_in_ref,
                          w_lr_ref, w_a2_ref, bs_ref, lr_stage_ref, lr_sem))
    step = functools.partial(
        _tile_step, pre_g_ref, b_a2_ref, ln_g_ref, ln_b_ref, ws_ref, bn_g_ref, post_g_ref,
        w_in_ref, w_out_ref, w_lr_ref, w_a2_ref, bs_ref, *tile_scratch)
    tile_spec = pl.BlockSpec((SEQ_TILE, D_MODEL), lambda b, s: (b * tiles_per_seq + s, 0))
    pltpu.emit_pipeline(
        lambda x_ref, o_ref: step(x_ref=x_ref, o_ref=o_ref),
        grid=(x_hbm.shape[0] // (SEQ_TILE * tiles_per_seq), tiles_per_seq),
        in_specs=[tile_spec], out_specs=[tile_spec])(x_hbm, o_hbm)


def _tile_step(pre_g_ref, b_a2_ref, ln_g_ref, ln_b_ref, ws_ref, bn_g_ref, post_g_ref,
               w_in_ref, w_out_ref, w_lr_ref, w_a2_ref, bs_ref,
               h_ref, vn_ref, ug_ref, qa_ref, qb_ref, ka_ref, kb_ref, q2_ref, k2_ref, v_ref,
               bg_ref, bc_ref, sprev_ref, mixed_ref, state_ref, *, x_ref, o_ref):
    tile = x_ref.shape[0]
    blocks = [slice(r0, r0 + GLA_BLOCK) for r0 in range(0, tile, GLA_BLOCK)]
    nt = (((1,), (1,)), ((), ()))
    tn = (((0,), (0,)), ((), ()))

    @pl.when(pl.program_id(1) == 0)
    def _reset_state():
        state_ref[...] = jnp.zeros_like(state_ref)

    def w_out_rows(r0, r1):
        return jnp.concatenate([w_out_ref[j] for j in range(r0 // WEIGHT_STAGE_ROWS,
                                                            r1 // WEIGHT_STAGE_ROWS)], axis=0)

    def proj(off, width):
        w = jnp.concatenate([w_in_ref[j] for j in range(off // WEIGHT_STAGE_ROWS,
                                                        (off + width) // WEIGHT_STAGE_ROWS)], axis=1)
        return jnp.dot(h_ref[...], w, preferred_element_type=F32)

    val = {}

    def gate_lowrank():
        val["lr"] = jnp.dot(h_ref[...], w_lr_ref[...],
                            preferred_element_type=F32).astype(BF16)

    def gate_log_decay():
        gate_logit = (jnp.dot(val["lr"], w_a2_ref[...], preferred_element_type=F32)
                      + b_a2_ref[...])
        val["log_a"] = _log_sigmoid(gate_logit) * (1.0 / B_GATE_TAU)

    def gla_q():
        val["q"] = proj(OFF_Q, D_QK) * (B_DK ** -0.5)

    def gla_k():
        val["k"] = proj(OFF_K, D_QK)

    r_t = lax.broadcasted_iota(jnp.int32, (GLA_BLOCK, GLA_BLOCK), 0)
    c_t = lax.broadcasted_iota(jnp.int32, (GLA_BLOCK, GLA_BLOCK), 1)
    shift = B_CHUNK.bit_length() - 1
    same_chunk_causal = (r_t >= c_t) & ((r_t >> shift) == (c_t >> shift))
    pair_causal = r_t >= c_t

    def gla_cumsum():
        cum_mat = same_chunk_causal.astype(BF16)
        cum_mat2 = jnp.concatenate([cum_mat, cum_mat], axis=1)
        log_a = val["log_a"]
        la_hi = log_a.astype(BF16)
        la_lo = (log_a - la_hi.astype(F32)).astype(BF16)
        for blk in blocks:
            bc_ref[blk, :] = jnp.dot(cum_mat2, jnp.concatenate([la_hi[blk], la_lo[blk]], axis=0),
                                     preferred_element_type=F32)

    def gla_v():
        v_ref[...] = proj(OFF_BV, D_B).astype(BF16)

    def gla_operands():
        q, k = val["q"], val["k"]
        zeros = jnp.zeros((B_CHUNK, D_QK), BF16)
        pair_decays = []
        for blk in blocks:
            first = slice(blk.start, blk.start + B_CHUNK)
            second = slice(blk.start + B_CHUNK, blk.stop)
            b_1, b_2 = bc_ref[first, :], bc_ref[second, :]
            b_last_1 = bc_ref[first.stop - 1:first.stop, :]
            b_last_2 = bc_ref[second.stop - 1:second.stop, :]
            b_mid_1 = bc_ref[first.start + B_CHUNK // 2 - 1:first.start + B_CHUNK // 2, :]
            b_mid_2 = bc_ref[second.start + B_CHUNK // 2 - 1:second.start + B_CHUNK // 2, :]
            q_1, q_2, k_1, k_2 = q[first], q[second], k[first], k[second]
            dec_1, dec_2 = jnp.exp(b_last_1), jnp.exp(b_last_2)
            ke_2 = (k_2 * jnp.exp(b_mid_2 - b_2)).astype(BF16)
            qa_ref[first, :] = (q_1 * jnp.exp(b_1 - b_mid_1)).astype(BF16)
            qa_ref[second, :] = zeros
            qb_ref[first, :] = zeros
            qb_ref[second, :] = (q_2 * jnp.exp(b_2 - b_mid_2)).astype(BF16)
            ka_ref[first, :] = (k_1 * jnp.exp(b_mid_1 - b_1)).astype(BF16)
            ka_ref[second, :] = ke_2
            kb_ref[first, :] = (k_1 * jnp.exp(b_mid_2 + (b_last_1 - b_1))).astype(BF16)
            kb_ref[second, :] = ke_2
            q2_ref[first, :] = (q_1 * jnp.exp(b_1)).astype(BF16)
            q2_ref[second, :] = (q_2 * jnp.exp(b_2) * dec_1).astype(BF16)
            k2_ref[first, :] = (k_1 * jnp.exp(b_last_1 - b_1) * dec_2).astype(BF16)
            k2_ref[second, :] = (k_2 * jnp.exp(b_last_2 - b_2)).astype(BF16)
            pair_decays.append(dec_1 * dec_2)
        val["pair_decays"] = pair_decays

    def gla_gate():
        bg_ref[...] = _silu(proj(OFF_BG, D_B))

    def gla_state():
        for hd in range(B_HEADS):
            kc_ = slice(hd * B_DK, (hd + 1) * B_DK)
            vc_ = slice(hd * B_DV, (hd + 1) * B_DV)
            s_t = state_ref[hd]
            for p, blk in enumerate(blocks):
                sprev_ref[p, hd] = s_t.T.astype(BF16)
                kv_t = lax.dot_general(v_ref[blk, vc_], k2_ref[blk, kc_], tn,
                                       preferred_element_type=F32)
                s_t = s_t * val["pair_decays"][p][:, kc_] + kv_t
            state_ref[hd] = s_t

    def gla_scores():
        scores = {}
        for blk in blocks:
            for hd in range(B_HEADS):
                kc_ = slice(hd * B_DK, (hd + 1) * B_DK)
                q_ab = jnp.concatenate([qa_ref[blk, kc_], qb_ref[blk, kc_]], axis=1)
                k_ab = jnp.concatenate([ka_ref[blk, kc_], kb_ref[blk, kc_]], axis=1)
                s_hd = lax.dot_general(q_ab, k_ab, nt, preferred_element_type=F32)
                scores[blk.start, hd] = jnp.where(pair_causal, s_hd, 0.0).astype(BF16)
        val["scores"] = scores

    def gla_output():
        for p, blk in enumerate(blocks):
            for hd in range(B_HEADS):
                kc_ = slice(hd * B_DK, (hd + 1) * B_DK)
                vc_ = slice(hd * B_DV, (hd + 1) * B_DV)
                lhs = jnp.concatenate([val["scores"][blk.start, hd], q2_ref[blk, kc_]], axis=1)
                rhs = jnp.concatenate([v_ref[blk, vc_], sprev_ref[p, hd]], axis=0)
                o = jnp.dot(lhs, rhs, preferred_element_type=F32)
                o_n = o * _rms_scale(o) * bn_g_ref[:, vc_]
                mixed_ref[blk, D_A + hd * B_DV:D_A + (hd + 1) * B_DV] = (
                    o_n * bg_ref[blk, vc_]).astype(BF16)

    def sgu_value():
        v_a = _gelu(proj(OFF_V, D_A))
        mu = jnp.mean(v_a, axis=-1, keepdims=True)
        v_c = v_a - mu
        v_n = v_c * lax.rsqrt(jnp.mean(v_c * v_c, axis=-1, keepdims=True) + EPS)
        vn_ref[...] = (v_n * ln_g_ref[...] + ln_b_ref[...]).astype(BF16)

    def sgu_u():
        pass

    def sgu_gate():
        for c0 in range(0, D_A, COL_BLOCK):
            u_blk = _gelu(proj(OFF_U + c0, COL_BLOCK))
            g_blk = _silu(proj(OFF_AG + c0, COL_BLOCK))
            ug_ref[:, c0:c0 + COL_BLOCK] = u_blk * g_blk

    def sgu_mix():
        row = lax.broadcasted_iota(jnp.int32, (A_CHUNK, A_CHUNK), 0)
        col = lax.broadcasted_iota(jnp.int32, (A_CHUNK, A_CHUNK), 1)
        causal_a = row >= col
        for g in range(A_GROUPS):
            w_g = jnp.where(causal_a, ws_ref[g], 0.0).astype(BF16)
            gc = slice(g * A_GROUP_DIM, (g + 1) * A_GROUP_DIM)
            for n in range(tile // A_CHUNK):
                rs = slice(n * A_CHUNK, (n + 1) * A_CHUNK)
                sp = jnp.dot(w_g, vn_ref[rs, gc], preferred_element_type=F32) + bs_ref[g]
                mixed_ref[rs, gc] = (ug_ref[rs, gc] * sp).astype(BF16)

    def out_a():
        val["y_a"] = jnp.dot(mixed_ref[:, :D_A], w_out_rows(0, D_A),
                             preferred_element_type=F32)

    def out_b():
        val["y_b"] = jnp.dot(mixed_ref[:, D_A:], w_out_rows(D_A, D_MIX),
                             preferred_element_type=F32)

    phases = dict(
        gate_lowrank=gate_lowrank, gate_log_decay=gate_log_decay, gla_q=gla_q, gla_k=gla_k,
        gla_cumsum=gla_cumsum, gla_v=gla_v, gla_operands=gla_operands, gla_gate=gla_gate,
        gla_state=gla_state, gla_scores=gla_scores, gla_output=gla_output,
        sgu_value=sgu_value, sgu_u=sgu_u, sgu_gate=sgu_gate, sgu_mix=sgu_mix,
        out_a=out_a, out_b=out_b)
    assert sorted(PHASE_ORDER) == sorted(phases)

    x = x_ref[...]
    h_ref[...] = (x * _rms_scale(x) * pre_g_ref[...]).astype(BF16)
    for name in PHASE_ORDER:
        phases[name]()
    y = val["y_a"] + val["y_b"]
    o_ref[...] = x_ref[...] + y * _rms_scale(y) * post_g_ref[...]


def _hybrid_layer(x, pre_g, w_in, w_a2, b_a2, ln_g, ln_b, w_s, b_s, bn_g, w_out, post_g):
    bsz, seq, d = x.shape
    assert d == D_MODEL and seq % SEQ_TILE == 0
    assert SEQ_TILE % A_CHUNK == 0 and SEQ_TILE % GLA_BLOCK == 0 and GLA_BLOCK == 2 * B_CHUNK
    assert w_in.shape == (D_MODEL, D_IN) and w_out.shape == (D_MIX, D_MODEL)
    assert OFF_LR % WEIGHT_STAGE_ROWS == 0 and D_MIX % WEIGHT_STAGE_ROWS == 0

    w_in_t = jnp.swapaxes(w_in, 0, 1).astype(F32)
    row_vec = lambda a: a.reshape(1, -1).astype(F32)

    hbm_spec = pl.BlockSpec(memory_space=pl.ANY)
    vmem_spec = pl.BlockSpec(memory_space=pltpu.VMEM)
    n_pairs = SEQ_TILE // GLA_BLOCK
    scratch = [
        pltpu.VMEM((OFF_LR // WEIGHT_STAGE_ROWS, D_MODEL, WEIGHT_STAGE_ROWS), BF16),
        pltpu.VMEM((D_MIX // WEIGHT_STAGE_ROWS, WEIGHT_STAGE_ROWS, D_MODEL), BF16),
        pltpu.VMEM((WEIGHT_STAGE_SLOTS, WEIGHT_STAGE_ROWS, D_MODEL), F32),
        pltpu.SemaphoreType.DMA((WEIGHT_STAGE_SLOTS,)),
        pltpu.VMEM((D_MODEL, LR_PAD), BF16),
        pltpu.VMEM((LR_PAD, D_QK), BF16),
        pltpu.VMEM((A_GROUPS, A_CHUNK, A_CHUNK), F32),
        pltpu.VMEM((LR_PAD, D_MODEL), F32),
        pltpu.SemaphoreType.DMA(()),
        pltpu.VMEM((SEQ_TILE, D_MODEL), BF16),
        pltpu.VMEM((SEQ_TILE, D_A), BF16),
        pltpu.VMEM((SEQ_TILE, D_A), F32),
        pltpu.VMEM((SEQ_TILE, D_QK), BF16),
        pltpu.VMEM((SEQ_TILE, D_QK), BF16),
        pltpu.VMEM((SEQ_TILE, D_QK), BF16),
        pltpu.VMEM((SEQ_TILE, D_QK), BF16),
        pltpu.VMEM((SEQ_TILE, D_QK), BF16),
        pltpu.VMEM((SEQ_TILE, D_QK), BF16),
        pltpu.VMEM((SEQ_TILE, D_B), BF16),
        pltpu.VMEM((SEQ_TILE, D_B), F32),
        pltpu.VMEM((SEQ_TILE, D_QK), F32),
        pltpu.VMEM((n_pairs, B_HEADS, B_DK, B_DV), BF16),
        pltpu.VMEM((SEQ_TILE, D_MIX), BF16),
        pltpu.VMEM((B_HEADS, B_DV, B_DK), F32),
    ]
    out = pl.pallas_call(
        functools.partial(_layer_kernel, tiles_per_seq=seq // SEQ_TILE),
        out_shape=jax.ShapeDtypeStruct((bsz * seq, d), x.dtype),
        in_specs=[hbm_spec, vmem_spec, hbm_spec] + [vmem_spec] * 7 + [hbm_spec, vmem_spec],
        out_specs=hbm_spec,
        scratch_shapes=scratch,
        compiler_params=pltpu.CompilerParams(vmem_limit_bytes=V7X_VMEM_LIMIT_BYTES),
        name="hybrid_gmlp_gla_layer",
    )(x.reshape(bsz * seq, d), row_vec(pre_g), w_in_t, w_a2.astype(F32), row_vec(b_a2),
      row_vec(ln_g), row_vec(ln_b), w_s.astype(F32), b_s.astype(F32), row_vec(bn_g),
      w_out.astype(F32), row_vec(post_g))
    return out.reshape(x.shape)


def kernel(x, pre_norm_g, w_in, w_a2, b_a2, a_ln_g, a_ln_b, a_w_s, a_b_s, b_norm_g, w_out, post_norm_g):
    for l in range(pre_norm_g.shape[0]):
        x = _hybrid_layer(x, pre_norm_g[l], w_in[l], w_a2[l], b_a2[l], a_ln_g[l], a_ln_b[l],
                          a_w_s[l], a_b_s[l], b_norm_g[l], w_out[l], post_norm_g[l])
    return x
```

```python
import functools

import jax
import jax.numpy as jnp
from jax import lax
from jax.experimental import pallas as pl
from jax.experimental.pallas import tpu as pltpu

D_MODEL = 1024
D_MIX = 2 * D_MODEL
D_A = D_MIX // 2
D_B = D_MIX - D_A
A_GROUPS = 8
A_GROUP_DIM = D_A // A_GROUPS
A_CHUNK = 128
B_HEADS = 4
B_DK = D_B // 2 // B_HEADS
B_DV = D_B // B_HEADS
B_GATE_RANK = 16
B_GATE_TAU = 16.0
B_CHUNK = 64
D_QK = B_HEADS * B_DK
EPS = 1e-6

OFF_U = 0
OFF_V = OFF_U + D_A
OFF_AG = OFF_V + D_A
OFF_Q = OFF_AG + D_A
OFF_K = OFF_Q + D_QK
OFF_BV = OFF_K + D_QK
OFF_BG = OFF_BV + D_B
OFF_LR = OFF_BG + D_B
D_IN = OFF_LR + B_GATE_RANK

V7X_LANES = 128
LR_PAD = V7X_LANES

SEQ_TILE = 512
GLA_BLOCK = 2 * B_CHUNK
COL_BLOCK = 512
V7X_VMEM_LIMIT_BYTES = 56 * 1024 * 1024
WEIGHT_STAGE_ROWS = 256
WEIGHT_STAGE_SLOTS = 8

PHASE_ORDER = (
    "gate_lowrank", "sgu_value", "gate_log_decay", "sgu_u", "sgu_gate", "sgu_mix",
    "gla_q", "gla_k", "gla_cumsum", "out_a", "gla_v", "gla_operands", "gla_gate",
    "gla_state", "gla_scores", "gla_output", "out_b",
)

F32 = jnp.float32
BF16 = jnp.bfloat16


def _gelu(x):
    return 0.5 * x * (1.0 + lax.erf(x * (0.5 ** 0.5)))


def _silu(x):
    half = 0.5 * x
    return half + half * jnp.tanh(half)


def _log_sigmoid(x):
    return jnp.minimum(x, 0.0) - jnp.log(1.0 + jnp.exp(-jnp.abs(x)))


def _rms_scale(x):
    return lax.rsqrt(jnp.mean(x * x, axis=-1, keepdims=True) + EPS)


def _load_weights_as_bf16(w_in_t_hbm, w_out_hbm, w_in_ref, w_out_ref, stage_ref, sem,
                          while_first_tiles_arrive):
    slots, rows = stage_ref.shape[:2]

    def load(src_hbm, dst_ref, convert, while_first_tiles_arrive=lambda: None):
        n_jobs = dst_ref.shape[0]

        def copy(i):
            start = i * rows if isinstance(i, int) else pl.multiple_of(i * rows, rows)
            return pltpu.make_async_copy(src_hbm.at[pl.ds(start, rows), :],
                                         stage_ref.at[i % slots], sem.at[i % slots])

        for i in range(min(slots - 1, n_jobs)):
            copy(i).start()
        while_first_tiles_arrive()

        def body(i, carry):
            @pl.when(i + slots - 1 < n_jobs)
            def _refill():
                copy(i + slots - 1).start()
            copy(i).wait()
            dst_ref[i] = convert(stage_ref[i % slots])
            return carry

        lax.fori_loop(0, n_jobs, body, 0)

    load(w_in_t_hbm, w_in_ref, lambda tile: tile.T.astype(BF16), while_first_tiles_arrive)
    load(w_out_hbm, w_out_ref, lambda tile: tile.astype(BF16))


def _prepare_small_operands(w_in_t_hbm, w_a2_in_ref, bs_in_ref, w_lr_ref, w_a2_ref, bs_ref,
                            lr_stage_ref, sem):
    lr_copy = pltpu.make_async_copy(w_in_t_hbm.at[pl.ds(OFF_LR, B_GATE_RANK), :],
                                    lr_stage_ref.at[pl.ds(0, B_GATE_RANK), :], sem)
    lr_copy.start()
    lr_stage_ref[B_GATE_RANK:, :] = jnp.zeros((LR_PAD - B_GATE_RANK, D_MODEL), F32)
    w_a2_ref[...] = jnp.zeros_like(w_a2_ref)
    w_a2_ref[:B_GATE_RANK, :] = w_a2_in_ref[...].astype(BF16)
    row = lax.broadcasted_iota(jnp.int32, (A_CHUNK, A_CHUNK), 0)
    col = lax.broadcasted_iota(jnp.int32, (A_CHUNK, A_CHUNK), 1)
    ones = jnp.ones((A_CHUNK, A_CHUNK), BF16)
    for g in range(A_GROUPS):
        rest = jnp.where(row == col, bs_in_ref[g:g + 1, :], 0.0)
        acc = jnp.zeros((A_CHUNK, A_CHUNK), F32)
        for _ in range(3):
            part = rest.astype(BF16)
            acc = acc + jnp.dot(part, ones, preferred_element_type=F32)
            rest = rest - part.astype(F32)
        bs_ref[g] = acc
    lr_copy.wait()
    w_lr_ref[...] = lr_stage_ref[...].T.astype(BF16)


def _layer_kernel(x_hbm, pre_g_ref, w_in_hbm, w_a2_in_ref, b_a2_ref, ln_g_ref, ln_b_ref,
                  ws_ref, bs_in_ref, bn_g_ref, w_out_hbm, post_g_ref, o_hbm,
                  w_in_ref, w_out_ref, stage_ref, stage_sem,
                  w_lr_ref, w_a2_ref, bs_ref, lr_stage_ref, lr_sem, *tile_scratch,
                  tiles_per_seq):
    _load_weights_as_bf16(
        w_in_hbm, w_out_hbm, w_in_ref, w_out_ref, stage_ref, stage_sem,
        functools.partial(_prepare_small_operands, w_in_hbm, w_a2_in_ref, bs_in_ref,
                          w_lr_ref, w_a2_ref, bs_ref, lr_stage_ref, lr_sem))
    step = functools.partial(
        _tile_step, pre_g_ref, b_a2_ref, ln_g_ref, ln_b_ref, ws_ref, bn_g_ref, post_g_ref,
        w_in_ref, w_out_ref, w_lr_ref, w_a2_ref, bs_ref, *tile_scratch)
    tile_spec = pl.BlockSpec((SEQ_TILE, D_MODEL), lambda b, s: (b * tiles_per_seq + s, 0))
    pltpu.emit_pipeline(
        lambda x_ref, o_ref: step(x_ref=x_ref, o_ref=o_ref),
        grid=(x_hbm.shape[0] // (SEQ_TILE * tiles_per_seq), tiles_per_seq),
        in_specs=[tile_spec], out_specs=[tile_spec], trace_scopes=False)(x_hbm, o_hbm)


def _tile_step(pre_g_ref, b_a2_ref, ln_g_ref, ln_b_ref, ws_ref, bn_g_ref, post_g_ref,
               w_in_ref, w_out_ref, w_lr_ref, w_a2_ref, bs_ref,
               h_ref, vn_ref, ug_ref, qa_ref, qb_ref, ka_ref, kb_ref, q2_ref, k2_ref, v_ref,
               bg_ref, bc_ref, sprev_ref, mixed_ref, state_ref, *, x_ref, o_ref):
    tile = x_ref.shape[0]
    blocks = [slice(r0, r0 + GLA_BLOCK) for r0 in range(0, tile, GLA_BLOCK)]
    nt = (((1,), (1,)), ((), ()))
    tn = (((0,), (0,)), ((), ()))

    @pl.when(pl.program_id(1) == 0)
    def _reset_state():
        state_ref[...] = jnp.zeros_like(state_ref)

    def w_out_rows(r0, r1):
        return jnp.concatenate([w_out_ref[j] for j in range(r0 // WEIGHT_STAGE_ROWS,
                                                            r1 // WEIGHT_STAGE_ROWS)], axis=0)

    def proj(off, width):
        w = jnp.concatenate([w_in_ref[j] for j in range(off // WEIGHT_STAGE_ROWS,
                                                        (off + width) // WEIGHT_STAGE_ROWS)], axis=1)
        return jnp.dot(h_ref[...], w, preferred_element_type=F32)

    val = {}

    def gate_lowrank():
        val["lr"] = jnp.dot(h_ref[...], w_lr_ref[...],
                            preferred_element_type=F32).astype(BF16)

    def gate_log_decay():
        gate_logit = (jnp.dot(val["lr"], w_a2_ref[...], preferred_element_type=F32)
                      + b_a2_ref[...])
        val["log_a"] = _log_sigmoid(gate_logit) * (1.0 / B_GATE_TAU)

    def gla_q():
        val["q"] = proj(OFF_Q, D_QK) * (B_DK ** -0.5)

    def gla_k():
        val["k"] = proj(OFF_K, D_QK)

    r_t = lax.broadcasted_iota(jnp.int32, (GLA_BLOCK, GLA_BLOCK), 0)
    c_t = lax.broadcasted_iota(jnp.int32, (GLA_BLOCK, GLA_BLOCK), 1)
    shift = B_CHUNK.bit_length() - 1
    same_chunk_causal = (r_t >= c_t) & ((r_t >> shift) == (c_t >> shift))
    pair_causal = r_t >= c_t

    def gla_cumsum():
        cum_mat = same_chunk_causal.astype(BF16)
        cum_mat2 = jnp.concatenate([cum_mat, cum_mat], axis=1)
        log_a = val["log_a"]
        la_hi = log_a.astype(BF16)
        la_lo = (log_a - la_hi.astype(F32)).astype(BF16)
        for blk in blocks:
            bc_ref[blk, :] = jnp.dot(cum_mat2, jnp.concatenate([la_hi[blk], la_lo[blk]], axis=0),
                                     preferred_element_type=F32)

    def gla_v():
        v_ref[...] = proj(OFF_BV, D_B).astype(BF16)

    def gla_operands():
        q, k = val["q"], val["k"]
        zeros = jnp.zeros((B_CHUNK, D_QK), BF16)
        pair_decays = []
        for blk in blocks:
            first = slice(blk.start, blk.start + B_CHUNK)
            second = slice(blk.start + B_CHUNK, blk.stop)
            b_1, b_2 = bc_ref[first, :], bc_ref[second, :]
            b_last_1 = bc_ref[first.stop - 1:first.stop, :]
            b_last_2 = bc_ref[second.stop - 1:second.stop, :]
            b_mid_1 = bc_ref[first.start + B_CHUNK // 2 - 1:first.start + B_CHUNK // 2, :]
            b_mid_2 = bc_ref[second.start + B_CHUNK // 2 - 1:second.start + B_CHUNK // 2, :]
            q_1, q_2, k_1, k_2 = q[first], q[second], k[first], k[second]
            dec_1, dec_2 = jnp.exp(b_last_1), jnp.exp(b_last_2)
            ke_2 = (k_2 * jnp.exp(b_mid_2 - b_2)).astype(BF16)
            qa_ref[first, :] = (q_1 * jnp.exp(b_1 - b_mid_1)).astype(BF16)
            qa_ref[second, :] = zeros
            qb_ref[first, :] = zeros
            qb_ref[second, :] = (q_2 * jnp.exp(b_2 - b_mid_2)).astype(BF16)
            ka_ref[first, :] = (k_1 * jnp.exp(b_mid_1 - b_1)).astype(BF16)
            ka_ref[second, :] = ke_2
            kb_ref[first, :] = (k_1 * jnp.exp(b_mid_2 + (b_last_1 - b_1))).astype(BF16)
            kb_ref[second, :] = ke_2
            q2_ref[first, :] = (q_1 * jnp.exp(b_1)).astype(BF16)
            q2_ref[second, :] = (q_2 * jnp.exp(b_2) * dec_1).astype(BF16)
            k2_ref[first, :] = (k_1 * jnp.exp(b_last_1 - b_1) * dec_2).astype(BF16)
            k2_ref[second, :] = (k_2 * jnp.exp(b_last_2 - b_2)).astype(BF16)
            pair_decays.append(dec_1 * dec_2)
        val["pair_decays"] = pair_decays

    def gla_gate():
        bg_ref[...] = _silu(proj(OFF_BG, D_B))

    def gla_state():
        for hd in range(B_HEADS):
            kc_ = slice(hd * B_DK, (hd + 1) * B_DK)
            vc_ = slice(hd * B_DV, (hd + 1) * B_DV)
            s_t = state_ref[hd]
            for p, blk in enumerate(blocks):
                sprev_ref[p, hd] = s_t.T.astype(BF16)
                kv_t = lax.dot_general(v_ref[blk, vc_], k2_ref[blk, kc_], tn,
                                       preferred_element_type=F32)
                s_t = s_t * val["pair_decays"][p][:, kc_] + kv_t
            state_ref[hd] = s_t

    def gla_scores():
        scores = {}
        for blk in blocks:
            for hd in range(B_HEADS):
                kc_ = slice(hd * B_DK, (hd + 1) * B_DK)
                q_ab = jnp.concatenate([qa_ref[blk, kc_], qb_ref[blk, kc_]], axis=1)
                k_ab = jnp.concatenate([ka_ref[blk, kc_], kb_ref[blk, kc_]], axis=1)
                s_hd = lax.dot_general(q_ab, k_ab, nt, preferred_element_type=F32)
                scores[blk.start, hd] = jnp.where(pair_causal, s_hd, 0.0).astype(BF16)
        val["scores"] = scores

    def gla_output():
        for p, blk in enumerate(blocks):
            for hd in range(B_HEADS):
                kc_ = slice(hd * B_DK, (hd + 1) * B_DK)
                vc_ = slice(hd * B_DV, (hd + 1) * B_DV)
                lhs = jnp.concatenate([val["scores"][blk.start, hd], q2_ref[blk, kc_]], axis=1)
                rhs = jnp.concatenate([v_ref[blk, vc_], sprev_ref[p, hd]], axis=0)
                o = jnp.dot(lhs, rhs, preferred_element_type=F32)
                o_n = o * _rms_scale(o) * bn_g_ref[:, vc_]
                mixed_ref[blk, D_A + hd * B_DV:D_A + (hd + 1) * B_DV] = (
                    o_n * bg_ref[blk, vc_]).astype(BF16)

    def sgu_value():
        v_a = _gelu(proj(OFF_V, D_A))
        mu = jnp.mean(v_a, axis=-1, keepdims=True)
        v_c = v_a - mu
        v_n = v_c * lax.rsqrt(jnp.mean(v_c * v_c, axis=-1, keepdims=True) + EPS)
        vn_ref[...] = (v_n * ln_g_ref[...] + ln_b_ref[...]).astype(BF16)

    def sgu_u():
        pass

    def sgu_gate():
        for c0 in range(0, D_A, COL_BLOCK):
            u_blk = _gelu(proj(OFF_U + c0, COL_BLOCK))
            g_blk = _silu(proj(OFF_AG + c0, COL_BLOCK))
            ug_ref[:, c0:c0 + COL_BLOCK] = u_blk * g_blk

    def sgu_mix():
        row = lax.broadcasted_iota(jnp.int32, (A_CHUNK, A_CHUNK), 0)
        col = lax.broadcasted_iota(jnp.int32, (A_CHUNK, A_CHUNK), 1)
        causal_a = row >= col
        for g in range(A_GROUPS):
            w_g = jnp.where(causal_a, ws_ref[g], 0.0).astype(BF16)
            gc = slice(g * A_GROUP_DIM, (g + 1) * A_GROUP_DIM)
            for n in range(tile // A_CHUNK):
                rs = slice(n * A_CHUNK, (n + 1) * A_CHUNK)
                sp = jnp.dot(w_g, vn_ref[rs, gc], preferred_element_type=F32) + bs_ref[g]
                mixed_ref[rs, gc] = (ug_ref[rs, gc] * sp).astype(BF16)

    def out_a():
        val["y_a"] = jnp.dot(mixed_ref[:, :D_A], w_out_rows(0, D_A),
                             preferred_element_type=F32)

    def out_b():
        val["y_b"] = jnp.dot(mixed_ref[:, D_A:], w_out_rows(D_A, D_MIX),
                             preferred_element_type=F32)

    phases = dict(
        gate_lowrank=gate_lowrank, gate_log_decay=gate_log_decay, gla_q=gla_q, gla_k=gla_k,
        gla_cumsum=gla_cumsum, gla_v=gla_v, gla_operands=gla_operands, gla_gate=gla_gate,
        gla_state=gla_state, gla_scores=gla_scores, gla_output=gla_output,
        sgu_value=sgu_value, sgu_u=sgu_u, sgu_gate=sgu_gate, sgu_mix=sgu_mix,
        out_a=out_a, out_b=out_b)
    assert sorted(PHASE_ORDER) == sorted(phases)

    x = x_ref[...]
    h_ref[...] = (x * _rms_scale(x) * pre_g_ref[...]).astype(BF16)
    for name in PHASE_ORDER:
        phases[name]()
    y = val["y_a"] + val["y_b"]
    o_ref[...] = x_ref[...] + y * _rms_scale(y) * post_g_ref[...]


def _hybrid_layer(x, pre_g, w_in, w_a2, b_a2, ln_g, ln_b, w_s, b_s, bn_g, w_out, post_g):
    bsz, seq, d = x.shape
    assert d == D_MODEL and seq % SEQ_TILE == 0
    assert SEQ_TILE % A_CHUNK == 0 and SEQ_TILE % GLA_BLOCK == 0 and GLA_BLOCK == 2 * B_CHUNK
    assert w_in.shape == (D_MODEL, D_IN) and w_out.shape == (D_MIX, D_MODEL)
    assert OFF_LR % WEIGHT_STAGE_ROWS == 0 and D_MIX % WEIGHT_STAGE_ROWS == 0

    w_in_t = jnp.swapaxes(w_in, 0, 1).astype(F32)
    row_vec = lambda a: a.reshape(1, -1).astype(F32)

    hbm_spec = pl.BlockSpec(memory_space=pl.ANY)
    vmem_spec = pl.BlockSpec(memory_space=pltpu.VMEM)
    n_pairs = SEQ_TILE // GLA_BLOCK
    scratch = [
        pltpu.VMEM((OFF_LR // WEIGHT_STAGE_ROWS, D_MODEL, WEIGHT_STAGE_ROWS), BF16),
        pltpu.VMEM((D_MIX // WEIGHT_STAGE_ROWS, WEIGHT_STAGE_ROWS, D_MODEL), BF16),
        pltpu.VMEM((WEIGHT_STAGE_SLOTS, WEIGHT_STAGE_ROWS, D_MODEL), F32),
        pltpu.SemaphoreType.DMA((WEIGHT_STAGE_SLOTS,)),
        pltpu.VMEM((D_MODEL, LR_PAD), BF16),
        pltpu.VMEM((LR_PAD, D_QK), BF16),
        pltpu.VMEM((A_GROUPS, A_CHUNK, A_CHUNK), F32),
        pltpu.VMEM((LR_PAD, D_MODEL), F32),
        pltpu.SemaphoreType.DMA(()),
        pltpu.VMEM((SEQ_TILE, D_MODEL), BF16),
        pltpu.VMEM((SEQ_TILE, D_A), BF16),
        pltpu.VMEM((SEQ_TILE, D_A), F32),
        pltpu.VMEM((SEQ_TILE, D_QK), BF16),
        pltpu.VMEM((SEQ_TILE, D_QK), BF16),
        pltpu.VMEM((SEQ_TILE, D_QK), BF16),
        pltpu.VMEM((SEQ_TILE, D_QK), BF16),
        pltpu.VMEM((SEQ_TILE, D_QK), BF16),
        pltpu.VMEM((SEQ_TILE, D_QK), BF16),
        pltpu.VMEM((SEQ_TILE, D_B), BF16),
        pltpu.VMEM((SEQ_TILE, D_B), F32),
        pltpu.VMEM((SEQ_TILE, D_QK), F32),
        pltpu.VMEM((n_pairs, B_HEADS, B_DK, B_DV), BF16),
        pltpu.VMEM((SEQ_TILE, D_MIX), BF16),
        pltpu.VMEM((B_HEADS, B_DV, B_DK), F32),
    ]
    out = pl.pallas_call(
        functools.partial(_layer_kernel, tiles_per_seq=seq // SEQ_TILE),
        out_shape=jax.ShapeDtypeStruct((bsz * seq, d), x.dtype),
        in_specs=[hbm_spec, vmem_spec, hbm_spec] + [vmem_spec] * 7 + [hbm_spec, vmem_spec],
        out_specs=hbm_spec,
        scratch_shapes=scratch,
        compiler_params=pltpu.CompilerParams(vmem_limit_bytes=V7X_VMEM_LIMIT_BYTES),
        name="hybrid_gmlp_gla_layer",
    )(x.reshape(bsz * seq, d), row_vec(pre_g), w_in_t, w_a2.astype(F32), row_vec(b_a2),
      row_vec(ln_g), row_vec(ln_b), w_s.astype(F32), b_s.astype(F32), row_vec(bn_g),
      w_out.astype(F32), row_vec(post_g))
    return out.reshape(x.shape)


def kernel(x, pre_norm_g, w_in, w_a2, b_a2, a_ln_g, a_ln_b, a_w_s, a_b_s, b_norm_g, w_out, post_norm_g):
    for l in range(pre_norm_g.shape[0]):
        x = _hybrid_layer(x, pre_norm_g[l], w_in[l], w_a2[l], b_a2[l], a_ln_g[l], a_ln_b[l],
                          a_w_s[l], a_b_s[l], b_norm_g[l], w_out[l], post_norm_g[l])
    return x
```
